```python
import math
import jax
import jax.numpy as jnp
from jax import lax
import numpy as np

D_MODEL = 1024
BATCH = 16
SEQ = 2048
DEPTH = 2

N_BRANCH = 4
BRANCH_WIDTH = D_MODEL // N_BRANCH
HEAD_DIM = 64
SB_HEADS = BRANCH_WIDTH // HEAD_DIM
RET_HEADS = BRANCH_WIDTH // HEAD_DIM
SSM_GROUP = 16
SSM_GROUPS = BRANCH_WIDTH // SSM_GROUP
SSM_STATE = 64
LRU_BLOCKS = 4
LRU_BLOCK = BRANCH_WIDTH // LRU_BLOCKS
CONV_WIDTH = 4
LRU_C = 8.0
Q_BLOCK = 128
RET_CHUNK = 128
ROPE_BASE = 10000.0
EPS = 1e-6
SPLITS = (3 * BRANCH_WIDTH, 4 * BRANCH_WIDTH, 7 * BRANCH_WIDTH, 8 * BRANCH_WIDTH, 12 * BRANCH_WIDTH)
IN_COLS = 12 * BRANCH_WIDTH + N_BRANCH * D_MODEL

kernel_name = 'hybrid_gated_sb_s5_retention_rglru'


def _rms_norm(x, g):
    xf = x.astype(jnp.float32)
    y = xf * lax.rsqrt(jnp.mean(xf * xf, axis=-1, keepdims=True) + EPS) * g.astype(jnp.float32)
    return y.astype(x.dtype)


def _split_heads(t, n_heads):
    b, s, _ = t.shape
    return t.reshape(b, s, n_heads, -1).transpose(0, 2, 1, 3)


def _merge_heads(t):
    b, h, s, d = t.shape
    return t.transpose(0, 2, 1, 3).reshape(b, s, h * d)


def _rotary(t):
    s, d = t.shape[-2], t.shape[-1]
    half = d // 2
    inv_freq = ROPE_BASE ** (-jnp.arange(half, dtype=jnp.float32) / half)
    ang = jnp.arange(s, dtype=jnp.float32)[:, None] * inv_freq[None, :]
    cos, sin = jnp.cos(ang), jnp.sin(ang)
    t1, t2 = t[..., :half], t[..., half:]
    return jnp.concatenate([t1 * cos - t2 * sin, t1 * sin + t2 * cos], axis=-1)


def stick_breaking_attention(q, k, v):
    b, hh, s, d = q.shape
    q = q.astype(jnp.float32) * (d ** -0.5)
    k = k.astype(jnp.float32)
    v = v.astype(jnp.float32)
    outs = []
    for blk in range(s // Q_BLOCK):
        lo = blk * Q_BLOCK
        hi = lo + Q_BLOCK
        z = jnp.einsum('bhqd,bhkd->bhqk', q[:, :, lo:hi], k[:, :, :hi])
        t_idx = lo + jnp.arange(Q_BLOCK)[:, None]
        s_idx = jnp.arange(hi)[None, :]
        mask = s_idx < t_idx
        log_fail = jnp.where(mask, jax.nn.log_sigmoid(-z), 0.0)
        after = lax.cumsum(log_fail, axis=3, reverse=True) - log_fail
        w = jnp.where(mask, jnp.exp(jax.nn.log_sigmoid(z) + after), 0.0)
        outs.append(jnp.einsum('bhqk,bhkd->bhqd', w, v[:, :, :hi]))
    return jnp.concatenate(outs, axis=2)


def retention(q, k, v):
    b, hh, s, d = q.shape
    q = _rotary(q.astype(jnp.float32))
    k = _rotary(k.astype(jnp.float32)) * (d ** -0.5)
    v = v.astype(jnp.float32)
    log_g = jnp.log1p(-(2.0 ** (-5.0 - jnp.arange(hh, dtype=jnp.float32))))
    nc = s // RET_CHUNK
    qc = q.reshape(b, hh, nc, RET_CHUNK, d)
    kc = k.reshape(b, hh, nc, RET_CHUNK, d)
    vc = v.reshape(b, hh, nc, RET_CHUNK, d)
    i = jnp.arange(RET_CHUNK, dtype=jnp.float32)
    rel = i[:, None] - i[None, :]
    decay = jnp.where(rel >= 0, jnp.exp(log_g[:, None, None] * jnp.maximum(rel, 0.0)), 0.0)
    scores = jnp.einsum('bhnid,bhnjd->bhnij', qc, kc) * decay[None, :, None]
    inner = jnp.einsum('bhnij,bhnje->bhnie', scores, vc)
    zeta = jnp.exp(log_g[:, None] * (RET_CHUNK - 1 - i)[None, :])
    xi = jnp.exp(log_g[:, None] * (i + 1.0)[None, :])
    kv = jnp.einsum('bhnjd,hj,bhnje->bhnde', kc, zeta, vc)
    chunk_decay = jnp.exp(log_g * RET_CHUNK)[:, None, None]

    def step(r, kv_n):
        return chunk_decay * r + kv_n, r

    _, r_prev = lax.scan(step, jnp.zeros((b, hh, d, d), jnp.float32), kv.transpose(2, 0, 1, 3, 4))
    r_prev = r_prev.transpose(1, 2, 0, 3, 4)
    cross = jnp.einsum('bhnid,hi,bhnde->bhnie', qc, xi, r_prev)
    o = (inner + cross).reshape(b, hh, s, d)
    mu = jnp.mean(o, axis=-1, keepdims=True)
    var = jnp.mean(jnp.square(o - mu), axis=-1, keepdims=True)
    return (o - mu) * lax.rsqrt(var + EPS)


def s5_branch(u, a_re, a_im, log_dt, b_re, b_im, c_re, c_im, d_skip, w_glu, b_glu):
    bsz, s, _ = u.shape
    u = u.astype(jnp.float32)
    ug = u.reshape(bsz, s, SSM_GROUPS, SSM_GROUP)
    a_re = a_re.astype(jnp.float32)
    a_im = a_im.astype(jnp.float32)
    dt = jnp.exp(log_dt.astype(jnp.float32))[:, None]
    mag = jnp.exp(dt * a_re)
    ab_re = mag * jnp.cos(dt * a_im)
    ab_im = mag * jnp.sin(dt * a_im)
    den = a_re * a_re + a_im * a_im
    num_re = ab_re - 1.0
    f_re = (num_re * a_re + ab_im * a_im) / den
    f_im = (ab_im * a_re - num_re * a_im) / den
    b_re = b_re.astype(jnp.float32)
    b_im = b_im.astype(jnp.float32)
    bb_re = f_re[..., None] * b_re - f_im[..., None] * b_im
    bb_im = f_re[..., None] * b_im + f_im[..., None] * b_re
    x_re = jnp.einsum('bsgc,gnc->sbgn', ug, bb_re)
    x_im = jnp.einsum('bsgc,gnc->sbgn', ug, bb_im)
    a_seq_re = jnp.broadcast_to(ab_re[None, None], (s, 1, SSM_GROUPS, SSM_STATE))
    a_seq_im = jnp.broadcast_to(ab_im[None, None], (s, 1, SSM_GROUPS, SSM_STATE))

    def combine(e1, e2):
        a1r, a1i, b1r, b1i = e1
        a2r, a2i, b2r, b2i = e2
        return (a2r * a1r - a2i * a1i, a2r * a1i + a2i * a1r,
                a2r * b1r - a2i * b1i + b2r, a2r * b1i + a2i * b1r + b2i)

    _, _, h_re, h_im = lax.associative_scan(combine, (a_seq_re, a_seq_im, x_re, x_im), axis=0)
    y = (jnp.einsum('sbgn,gcn->bsgc', h_re, c_re.astype(jnp.float32))
         - jnp.einsum('sbgn,gcn->bsgc', h_im, c_im.astype(jnp.float32)))
    y = y.reshape(bsz, s, BRANCH_WIDTH) + d_skip.astype(jnp.float32) * u
    y = jax.nn.gelu(y)
    return y * jax.nn.sigmoid(y @ w_glu.astype(jnp.float32) + b_glu.astype(jnp.float32))


def rg_lru_branch(u, conv_w, conv_b, w_a, b_a, w_x, b_x, lam):
    bsz, s, _ = u.shape
    u = u.astype(jnp.float32)
    xc = lax.conv_general_dilated(
        u, conv_w.astype(jnp.float32)[:, None, :], window_strides=(1,),
        padding=[(CONV_WIDTH - 1, 0)], dimension_numbers=('NWC', 'WIO', 'NWC'),
        feature_group_count=BRANCH_WIDTH) + conv_b.astype(jnp.float32)
    xb = xc.reshape(bsz, s, LRU_BLOCKS, LRU_BLOCK)
    r = jax.nn.sigmoid(jnp.einsum('bsnc,ncd->bsnd', xb, w_a.astype(jnp.float32)) + b_a.astype(jnp.float32))
    ig = jax.nn.sigmoid(jnp.einsum('bsnc,ncd->bsnd', xb, w_x.astype(jnp.float32)) + b_x.astype(jnp.float32))
    r = r.reshape(bsz, s, BRANCH_WIDTH)
    ig = ig.reshape(bsz, s, BRANCH_WIDTH)
    log_a = -LRU_C * r * jax.nn.softplus(-lam.astype(jnp.float32))
    a = jnp.exp(log_a)
    bterm = jnp.sqrt(-jnp.expm1(2.0 * log_a)) * (ig * xc)

    def combine(e1, e2):
        a1, b1 = e1
        a2, b2 = e2
        return a2 * a1, a2 * b1 + b2

    _, h = lax.associative_scan(combine, (a, bterm), axis=1)
    return h


def _layer(x, pre_g, post_g, w_in, ssm_a_re, ssm_a_im, ssm_log_dt, ssm_b_re, ssm_b_im,
           ssm_c_re, ssm_c_im, ssm_d, ssm_w_glu, ssm_b_glu, lru_conv_w, lru_conv_b,
           lru_w_a, lru_b_a, lru_w_x, lru_b_x, lru_lambda, w_branch, w_out):
    dtype = x.dtype
    bsz, s, _ = x.shape
    h = _rms_norm(x, pre_g)
    proj = jnp.einsum('bsd,dc->bsc', h, w_in)
    sb_qkv, ssm_u, ret_qkv, lru_u, gates, merge = jnp.split(proj, SPLITS, axis=-1)
    q, k, v = jnp.split(sb_qkv, 3, axis=-1)
    y_sb = _merge_heads(stick_breaking_attention(
        _split_heads(q, SB_HEADS), _split_heads(k, SB_HEADS), _split_heads(v, SB_HEADS)))
    y_ssm = s5_branch(ssm_u, ssm_a_re, ssm_a_im, ssm_log_dt, ssm_b_re, ssm_b_im,
                      ssm_c_re, ssm_c_im, ssm_d, ssm_w_glu, ssm_b_glu)
    q, k, v = jnp.split(ret_qkv, 3, axis=-1)
    y_ret = _merge_heads(retention(
        _split_heads(q, RET_HEADS), _split_heads(k, RET_HEADS), _split_heads(v, RET_HEADS)))
    y_lru = rg_lru_branch(lru_u, lru_conv_w, lru_conv_b, lru_w_a, lru_b_a, lru_w_x, lru_b_x, lru_lambda)
    mix = jnp.stack([y_sb, y_ssm, y_ret, y_lru], axis=2).astype(dtype)
    mix = mix * jax.nn.silu(gates.reshape(bsz, s, N_BRANCH, BRANCH_WIDTH))
    y = jnp.einsum('bsnw,nwd->bsnd', mix, w_branch)
    gate = jax.nn.sigmoid(merge.reshape(bsz, s, N_BRANCH, D_MODEL))
    merged = jnp.sum(gate * y, axis=2)
    out = jnp.einsum('bsd,de->bse', merged, w_out)
    return x + _rms_norm(out, post_g)


def setup_inputs(seed: int = 0) -> dict:
    key = jax.random.key(seed)
    ks = jax.random.split(key, 24)
    f32 = jnp.float32
    W, G, N, C = BRANCH_WIDTH, SSM_GROUPS, SSM_STATE, SSM_GROUP
    nrm = lambda k, shape, scale: jax.random.normal(k, shape, f32) * scale
    x = jax.random.normal(ks[0], (BATCH, SEQ, D_MODEL), f32)
    pre_norm_g = 1.0 + nrm(ks[1], (DEPTH, D_MODEL), 0.02)
    post_norm_g = 1.0 + nrm(ks[2], (DEPTH, D_MODEL), 0.02)
    w_in = nrm(ks[3], (DEPTH, D_MODEL, IN_COLS), D_MODEL ** -0.5)
    ssm_a_re = -0.5 + nrm(ks[4], (DEPTH, G, N), 0.01)
    ssm_a_im = jnp.pi * jnp.arange(N, dtype=f32)[None, None, :] + nrm(ks[5], (DEPTH, G, N), 0.01)
    ssm_log_dt = jax.random.uniform(ks[6], (DEPTH, G), f32, minval=math.log(1e-3), maxval=math.log(1e-1))
    ssm_b_re = nrm(ks[7], (DEPTH, G, N, C), (2.0 * C) ** -0.5)
    ssm_b_im = nrm(ks[8], (DEPTH, G, N, C), (2.0 * C) ** -0.5)
    ssm_c_re = nrm(ks[9], (DEPTH, G, C, N), (2.0 * N) ** -0.5)
    ssm_c_im = nrm(ks[10], (DEPTH, G, C, N), (2.0 * N) ** -0.5)
    ssm_d = nrm(ks[11], (DEPTH, W), 1.0)
    ssm_w_glu = nrm(ks[12], (DEPTH, W, W), W ** -0.5)
    ssm_b_glu = nrm(ks[13], (DEPTH, W), 0.02)
    lru_conv_w = nrm(ks[14], (DEPTH, CONV_WIDTH, W), CONV_WIDTH ** -0.5)
    lru_conv_b = nrm(ks[15], (DEPTH, W), 0.02)
    lru_w_a = nrm(ks[16], (DEPTH, LRU_BLOCKS, LRU_BLOCK, LRU_BLOCK), LRU_BLOCK ** -0.5)
    lru_b_a = nrm(ks[17], (DEPTH, LRU_BLOCKS, LRU_BLOCK), 0.02)
    lru_w_x = nrm(ks[18], (DEPTH, LRU_BLOCKS, LRU_BLOCK, LRU_BLOCK), LRU_BLOCK ** -0.5)
    lru_b_x = nrm(ks[19], (DEPTH, LRU_BLOCKS, LRU_BLOCK), 0.02)
    a_c = jax.random.uniform(ks[20], (DEPTH, W), f32, minval=0.9, maxval=0.999)
    a0 = a_c ** (1.0 / LRU_C)
    lru_lambda = jnp.log(a0) - jnp.log1p(-a0)
    w_branch = nrm(ks[21], (DEPTH, N_BRANCH, W, D_MODEL), W ** -0.5)
    w_out = nrm(ks[22], (DEPTH, D_MODEL, D_MODEL), D_MODEL ** -0.5)
    return {'x': x, 'pre_norm_g': pre_norm_g, 'post_norm_g': post_norm_g, 'w_in': w_in,
            'ssm_a_re': ssm_a_re, 'ssm_a_im': ssm_a_im, 'ssm_log_dt': ssm_log_dt,
            'ssm_b_re': ssm_b_re, 'ssm_b_im': ssm_b_im, 'ssm_c_re': ssm_c_re, 'ssm_c_im': ssm_c_im,
            'ssm_d': ssm_d, 'ssm_w_glu': ssm_w_glu, 'ssm_b_glu': ssm_b_glu,
            'lru_conv_w': lru_conv_w, 'lru_conv_b': lru_conv_b, 'lru_w_a': lru_w_a, 'lru_b_a': lru_b_a,
            'lru_w_x': lru_w_x, 'lru_b_x': lru_b_x, 'lru_lambda': lru_lambda,
            'w_branch': w_branch, 'w_out': w_out}


def reference(x, pre_norm_g, post_norm_g, w_in, ssm_a_re, ssm_a_im, ssm_log_dt, ssm_b_re,
              ssm_b_im, ssm_c_re, ssm_c_im, ssm_d, ssm_w_glu, ssm_b_glu, lru_conv_w, lru_conv_b,
              lru_w_a, lru_b_a, lru_w_x, lru_b_x, lru_lambda, w_branch, w_out):
    for l in range(DEPTH):
        x = _layer(x, pre_norm_g[l], post_norm_g[l], w_in[l], ssm_a_re[l], ssm_a_im[l],
                   ssm_log_dt[l], ssm_b_re[l], ssm_b_im[l], ssm_c_re[l], ssm_c_im[l], ssm_d[l],
                   ssm_w_glu[l], ssm_b_glu[l], lru_conv_w[l], lru_conv_b[l], lru_w_a[l], lru_b_a[l],
                   lru_w_x[l], lru_b_x[l], lru_lambda[l], w_branch[l], w_out[l])
    return x
```

```python
import functools
import math

import jax
import jax.numpy as jnp
from jax import lax
from jax.experimental import pallas as pl
from jax.experimental.pallas import tpu as pltpu

F32 = jnp.float32
BF16 = jnp.bfloat16

D_MODEL = 1024
N_BRANCH = 4
WIDTH = D_MODEL // N_BRANCH
HEAD_DIM = 64
N_HEADS = WIDTH // HEAD_DIM
SSM_GROUPS = 16
SSM_GROUP = 16
SSM_STATE = 64
SSM_LANES = SSM_GROUPS * SSM_STATE
LRU_BLOCKS = 4
LRU_BLOCK = 64
CONV_WIDTH = 4
LRU_C = 8.0
ROPE_BASE = 10000.0
EPS = 1e-6
MIX_COLS = 8 * WIDTH
GATE_COLS = N_BRANCH * WIDTH
IN_BLOCKS = MIX_COLS // WIDTH

ROW_TILE = 512
ATT_TILE = 256
RET_TILE = 256
SCAN_STEPS = 64
SCAN_LANES = 512
VMEM_LIMIT = 56 * 1024 * 1024


def _dot(a, b):
    return jnp.dot(a, b, preferred_element_type=F32)


def _dot_nt(a, b):
    return lax.dot_general(a, b, (((1,), (1,)), ((), ())), preferred_element_type=F32)


def _sigmoid(x):
    return 0.5 * jnp.tanh(0.5 * x) + 0.5


def _rms_norm(x, g):
    return x * lax.rsqrt(jnp.mean(x * x, axis=-1, keepdims=True) + EPS) * g


def _const_spec(shape):
    nd = len(shape)
    return pl.BlockSpec(shape, lambda *_: (0,) * nd, pipeline_mode=pl.Buffered(1))


def _params(*sem):
    return pltpu.CompilerParams(dimension_semantics=sem, vmem_limit_bytes=VMEM_LIMIT)


def _inproj_body(x_ref, g_ref, w_ref, o_ref):
    h = _rms_norm(x_ref[...], g_ref[...]).astype(BF16)
    o_ref[...] = _dot(h, w_ref[...]).astype(o_ref.dtype)


def _inproj(x2, g, w_mix):
    rows = x2.shape[0]
    return pl.pallas_call(
        _inproj_body,
        out_shape=jax.ShapeDtypeStruct((rows, MIX_COLS), BF16),
        grid=(rows // ROW_TILE,),
        in_specs=[pl.BlockSpec((ROW_TILE, D_MODEL), lambda i: (i, 0)),
                  _const_spec((1, D_MODEL)),
                  _const_spec((D_MODEL, MIX_COLS))],
        out_specs=pl.BlockSpec((ROW_TILE, MIX_COLS), lambda i: (i, 0)),
        compiler_params=_params("parallel"),
        name="inproj",
    )(x2, g, w_mix)


def _sb_body(q_ref, k_ref, v_ref, tri_ref, o_ref, acc_ref):
    i = pl.program_id(1)
    tq = q_ref.shape[0]
    q = q_ref[...]
    lane_head = lax.broadcasted_iota(jnp.int32, (1, WIDTH), 1) // HEAD_DIM
    row = lax.broadcasted_iota(jnp.int32, (tq, tq), 0)
    col = lax.broadcasted_iota(jnp.int32, (tq, tq), 1)
    causal = col < row
    tri = tri_ref[...]
    acc_ref[...] = jnp.zeros_like(acc_ref)

    def block(qh, head_mask, j, carry, diagonal):
        k0 = pl.multiple_of(j * tq, tq)
        z = _dot_nt(qh, k_ref[pl.ds(k0, tq), :])
        log_pass = jnp.minimum(z, 0.0) - jnp.log(1.0 + jnp.exp(-jnp.abs(z)))
        log_fail = log_pass - z
        if diagonal:
            log_fail = jnp.where(causal, log_fail, 0.0)
        hi = log_fail.astype(BF16)
        lo = (log_fail - hi.astype(F32)).astype(BF16)
        after = _dot(hi, tri) + _dot(lo, tri) + carry
        w = jnp.exp(log_pass + after)
        if diagonal:
            w = jnp.where(causal, w, 0.0)
        vh = jnp.where(head_mask, v_ref[pl.ds(k0, tq), :], jnp.zeros((), BF16))
        acc_ref[...] += _dot(w.astype(BF16), vh)
        return carry + jnp.sum(log_fail, axis=-1, keepdims=True)

    for h in range(N_HEADS):
        head_mask = lane_head == h
        qh = jnp.where(head_mask, q, jnp.zeros((), BF16)) * jnp.asarray(HEAD_DIM ** -0.5, BF16)
        carry = block(qh, head_mask, i, jnp.zeros((tq, 1), F32), True)
        lax.fori_loop(0, i, lambda jj, c: block(qh, head_mask, i - 1 - jj, c, False), carry)
    o_ref[...] = acc_ref[...].astype(o_ref.dtype)


def _stick_breaking(mix_sb, tri, seq, batch):
    return pl.pallas_call(
        _sb_body,
        out_shape=jax.ShapeDtypeStruct((seq, batch * WIDTH), BF16),
        grid=(batch, seq // ATT_TILE),
        in_specs=[pl.BlockSpec((ATT_TILE, WIDTH), lambda b, i: (i, b * IN_BLOCKS + 0)),
                  pl.BlockSpec((seq, WIDTH), lambda b, i: (0, b * IN_BLOCKS + 1)),
                  pl.BlockSpec((seq, WIDTH), lambda b, i: (0, b * IN_BLOCKS + 2)),
                  _const_spec((ATT_TILE, ATT_TILE))],
        out_specs=pl.BlockSpec((ATT_TILE, WIDTH), lambda b, i: (i, b)),
        scratch_shapes=[pltpu.VMEM((ATT_TILE, WIDTH), F32)],
        compiler_params=_params("parallel", "arbitrary"),
        name="stick_breaking",
    )(mix_sb, mix_sb, mix_sb, tri)


def _ret_body(q_ref, k_ref, v_ref, cos_ref, sin_ref, rot_ref, decay_ref, xi_ref, zeta_ref, cd_ref,
              bd_ref, avg_ref, o_ref, state_ref):
    @pl.when(pl.program_id(1) == 0)
    def _():
        state_ref[...] = jnp.zeros_like(state_ref)

    q = q_ref[...]
    k = k_ref[...]
    v = v_ref[...]
    cosf = cos_ref[...]
    sinf = sin_ref[...]
    rot = rot_ref[...]
    qr = q.astype(F32) * cosf + _dot(q, rot) * sinf
    kr = (k.astype(F32) * cosf + _dot(k, rot) * sinf) * (HEAD_DIM ** -0.5)
    qrb = qr.astype(BF16)
    krb = kr.astype(BF16)
    lane_head = lax.broadcasted_iota(jnp.int32, (1, WIDTH), 1) // HEAD_DIM
    zero = jnp.zeros((), BF16)
    o = None
    for h in range(N_HEADS):
        head_mask = lane_head == h
        s = _dot_nt(jnp.where(head_mask, qrb, zero), krb) * decay_ref[h]
        part = _dot(s.astype(BF16), jnp.where(head_mask, v, zero))
        o = part if o is None else o + part
    state = state_ref[...]
    o = o + _dot((qr * xi_ref[...]).astype(BF16), state.astype(BF16))
    kv = _dot((kr * zeta_ref[...]).T.astype(BF16), v)
    state_ref[...] = cd_ref[...] * state + kv * bd_ref[...]
    avg = avg_ref[...]
    d = o - _dot(o.astype(BF16), avg)
    var = _dot((d * d).astype(BF16), avg)
    o_ref[...] = (d * lax.rsqrt(var + EPS)).astype(o_ref.dtype)


def _retention(mix_sb, tabs, seq, batch):
    c = RET_TILE
    tile = lambda col: pl.BlockSpec((c, WIDTH), lambda b, n: (n, b * IN_BLOCKS + col))
    return pl.pallas_call(
        _ret_body,
        out_shape=jax.ShapeDtypeStruct((seq, batch * WIDTH), BF16),
        grid=(batch, seq // c),
        in_specs=[tile(4), tile(5), tile(6),
                  pl.BlockSpec((c, WIDTH), lambda b, n: (n, 0)),
                  pl.BlockSpec((c, WIDTH), lambda b, n: (n, 0)),
                  _const_spec((WIDTH, WIDTH)),
                  _const_spec((N_HEADS, c, c)),
                  _const_spec((c, WIDTH)),
                  _const_spec((c, WIDTH)),
                  _const_spec((WIDTH, 1)),
                  _const_spec((WIDTH, WIDTH)),
                  _const_spec((WIDTH, WIDTH))],
        out_specs=pl.BlockSpec((c, WIDTH), lambda b, n: (n, b)),
        scratch_shapes=[pltpu.VMEM((WIDTH, WIDTH), F32)],
        compiler_params=_params("parallel", "arbitrary"),
        name="retention",
    )(mix_sb, mix_sb, mix_sb, tabs["cos"], tabs["sin"], tabs["rot"], tabs["decay"], tabs["xi"],
      tabs["zeta"], tabs["cd"], tabs["bd"], tabs["avg"])


def _retention_tables(seq):
    c = RET_TILE
    half = HEAD_DIM // 2
    inv_freq = ROPE_BASE ** (-jnp.arange(half, dtype=F32) / half)
    ang = jnp.arange(seq, dtype=F32)[:, None] * inv_freq[None, :]
    cos = jnp.tile(jnp.cos(ang), (1, 2 * N_HEADS))
    sin = jnp.tile(jnp.sin(ang), (1, 2 * N_HEADS))
    lane = jnp.arange(WIDTH)
    d = lane % HEAD_DIM
    src = jnp.where(d < half, lane + half, lane - half)
    sign = jnp.where(d < half, -1.0, 1.0)
    rot = (jnp.zeros((WIDTH, WIDTH), F32).at[src, lane].set(sign)).astype(BF16)
    log_g = jnp.log1p(-(2.0 ** (-5.0 - jnp.arange(N_HEADS, dtype=F32))))
    i = jnp.arange(c, dtype=F32)
    rel = i[:, None] - i[None, :]
    decay = jnp.where(rel >= 0, jnp.exp(log_g[:, None, None] * jnp.maximum(rel, 0.0)), 0.0)
    log_g_lane = log_g[lane // HEAD_DIM]
    xi = jnp.exp(log_g_lane[None, :] * (i + 1.0)[:, None])
    zeta = jnp.exp(log_g_lane[None, :] * (c - 1 - i)[:, None])
    cd = jnp.exp(log_g_lane * c)[:, None]
    same_head = (lane[:, None] // HEAD_DIM) == (lane[None, :] // HEAD_DIM)
    return {"cos": cos, "sin": sin, "rot": rot, "decay": decay, "xi": xi, "zeta": zeta, "cd": cd,
            "bd": same_head.astype(F32), "avg": (same_head.astype(F32) / HEAD_DIM).astype(BF16)}


def _s5_prep_body(a_re_ref, a_im_ref, log_dt_ref, b_re_ref, b_im_ref, lam_re_ref, lam_im_ref,
                  bb_re_ref, bb_im_ref):
    a_re = a_re_ref[...]
    a_im = a_im_ref[...]
    dt = jnp.exp(log_dt_ref[...])
    mag = jnp.exp(dt * a_re)
    ab_re = mag * jnp.cos(dt * a_im)
    ab_im = mag * jnp.sin(dt * a_im)
    den = a_re * a_re + a_im * a_im
    num_re = ab_re - 1.0
    f_re = (num_re * a_re + ab_im * a_im) / den
    f_im = (ab_im * a_re - num_re * a_im) / den
    b_re = b_re_ref[...]
    b_im = b_im_ref[...]
    lam_re_ref[...] = ab_re
    lam_im_ref[...] = ab_im
    bb_re_ref[...] = f_re * b_re - f_im * b_im
    bb_im_ref[...] = f_re * b_im + f_im * b_re


def _s5_prep(a_re, a_im, log_dt, b_re, b_im):
    rows = a_re.shape[0]
    col = jax.ShapeDtypeStruct((rows, 1), F32)
    mat = jax.ShapeDtypeStruct((rows, SSM_GROUP), F32)
    return pl.pallas_call(_s5_prep_body, out_shape=(col, col, mat, mat), name="s5_prep")(
        a_re, a_im, log_dt, b_re, b_im)


def _gelu_tanh(y):
    return 0.5 * y * (1.0 + jnp.tanh(math.sqrt(2.0 / math.pi) * (y + 0.044715 * (y * y * y))))


def _scan_body(us_ref, ul_ref, bbd_ref, lam_re_ref, lam_im_ref, c_re_ref, c_im_ref, dskip_ref, wglu_ref,
               bglu_ref, convw_ref, convb_ref, wa_ref, ba_ref, wx_ref, bx_ref, lam_lru_ref,
               ys_ref, yl_ref,
               x_scr, hb_scr, hstate, ext_scr, a_scr, b_scr, lstate, *, batch):
    rows = us_ref.shape[0]
    steps = rows // batch
    tail = (CONV_WIDTH - 1) * batch

    @pl.when(pl.program_id(0) == 0)
    def _():
        hstate[...] = jnp.zeros_like(hstate)
        lstate[...] = jnp.zeros_like(lstate)
        ext_scr[0:tail, :] = jnp.zeros((tail, WIDTH), F32)

    us = us_ref[...]
    x_scr[...] = _dot(us, bbd_ref[...])
    for c in range(SSM_LANES // SCAN_LANES):
        re = slice(c * SCAN_LANES, (c + 1) * SCAN_LANES)
        im = slice(SSM_LANES + c * SCAN_LANES, SSM_LANES + (c + 1) * SCAN_LANES)
        lam_re = lam_re_ref[:, re]
        lam_im = lam_im_ref[:, re]

        def step(t, carry, re=re, im=im, lam_re=lam_re, lam_im=lam_im):
            h_re, h_im = carry
            r0 = pl.multiple_of(t * batch, batch)
            n_re = lam_re * h_re - lam_im * h_im + x_scr[pl.ds(r0, batch), re]
            n_im = lam_re * h_im + lam_im * h_re + x_scr[pl.ds(r0, batch), im]
            hb_scr[pl.ds(r0, batch), re] = n_re.astype(BF16)
            hb_scr[pl.ds(r0, batch), im] = n_im.astype(BF16)
            return n_re, n_im

        h_re, h_im = lax.fori_loop(0, steps, step, (hstate[:, re], hstate[:, im]), unroll=2)
        hstate[:, re] = h_re
        hstate[:, im] = h_im
    y = _dot(hb_scr[:, 0:SSM_LANES], c_re_ref[...]) - _dot(hb_scr[:, SSM_LANES:], c_im_ref[...])
    y = _gelu_tanh(y + dskip_ref[...] * us.astype(F32))
    ys_ref[...] = (y * _sigmoid(_dot(y.astype(BF16), wglu_ref[...]) + bglu_ref[...])).astype(ys_ref.dtype)

    ext_scr[tail:tail + rows, :] = ul_ref[...].astype(F32)
    xc = convb_ref[...] + convw_ref[0:1, :] * ext_scr[0:rows, :]
    for kk in range(1, CONV_WIDTH):
        xc = xc + convw_ref[kk:kk + 1, :] * ext_scr[kk * batch:kk * batch + rows, :]
    ext_scr[0:tail, :] = ext_scr[rows:rows + tail, :]
    xcb = xc.astype(BF16)
    r = _sigmoid(_dot(xcb, wa_ref[...]) + ba_ref[...])
    ig = _sigmoid(_dot(xcb, wx_ref[...]) + bx_ref[...])
    neg_lam = -lam_lru_ref[...]
    softplus = jnp.maximum(neg_lam, 0.0) + jnp.log(1.0 + jnp.exp(-jnp.abs(neg_lam)))
    a = jnp.exp(-LRU_C * r * softplus)
    a_scr[...] = a
    b_scr[...] = jnp.sqrt(1.0 - a * a) * (ig * xc)

    def lru_step(t, h):
        r0 = pl.multiple_of(t * batch, batch)
        h = a_scr[pl.ds(r0, batch), :] * h + b_scr[pl.ds(r0, batch), :]
        yl_ref[pl.ds(r0, batch), :] = h.astype(yl_ref.dtype)
        return h

    lstate[...] = lax.fori_loop(0, steps, lru_step, lstate[...], unroll=4)


def _recurrent(mix_tm, p, batch):
    rows_total = mix_tm.shape[0]
    rows = SCAN_STEPS * batch
    tail = (CONV_WIDTH - 1) * batch
    vec = lambda: _const_spec((1, WIDTH))
    sq = lambda: _const_spec((WIDTH, WIDTH))
    out = jax.ShapeDtypeStruct((rows_total, WIDTH), BF16)
    return pl.pallas_call(
        functools.partial(_scan_body, batch=batch),
        out_shape=(out, out),
        grid=(rows_total // rows,),
        in_specs=[pl.BlockSpec((rows, WIDTH), lambda t: (t, 3)),
                  pl.BlockSpec((rows, WIDTH), lambda t: (t, 7)),
                  _const_spec((WIDTH, 2 * SSM_LANES)),
                  _const_spec((batch, SSM_LANES)), _const_spec((batch, SSM_LANES)),
                  _const_spec((SSM_LANES, WIDTH)), _const_spec((SSM_LANES, WIDTH)),
                  vec(), sq(), vec(),
                  _const_spec((CONV_WIDTH, WIDTH)), vec(), sq(), vec(), sq(), vec(), vec()],
        out_specs=(pl.BlockSpec((rows, WIDTH), lambda t: (t, 0)),
                   pl.BlockSpec((rows, WIDTH), lambda t: (t, 0))),
        scratch_shapes=[pltpu.VMEM((rows, 2 * SSM_LANES), F32),
                        pltpu.VMEM((rows, 2 * SSM_LANES), BF16),
                        pltpu.VMEM((batch, 2 * SSM_LANES), F32),
                        pltpu.VMEM((rows + tail, WIDTH), F32),
                        pltpu.VMEM((rows, WIDTH), F32),
                        pltpu.VMEM((rows, WIDTH), F32),
                        pltpu.VMEM((batch, WIDTH), F32)],
        compiler_params=_params("arbitrary"),
        name="recurrent",
    )(mix_tm, mix_tm, p["bbd"], p["lam_re"], p["lam_im"], p["c_re"], p["c_im"], p["d_skip"], p["w_glu"],
      p["b_glu"], p["conv_w"], p["conv_b"], p["w_a"], p["b_a"], p["w_x"], p["b_x"], p["lam"])


def _merge_body(x_ref, pre_g_ref, post_g_ref, y0_ref, y1_ref, y2_ref, y3_ref, wg_ref, wm_ref, wb_ref,
                wo_ref, o_ref):
    x = x_ref[...]
    h = _rms_norm(x, pre_g_ref[...]).astype(BF16)
    merged = None
    for n, y_ref in enumerate((y0_ref, y1_ref, y2_ref, y3_ref)):
        g = _dot(h, wg_ref[:, n * WIDTH:(n + 1) * WIDTH])
        mix = y_ref[...].astype(F32) * (g * _sigmoid(g))
        branch = _dot(mix.astype(BF16), wb_ref[n])
        gate = _sigmoid(_dot(h, wm_ref[:, n * D_MODEL:(n + 1) * D_MODEL]))
        merged = gate * branch if merged is None else merged + gate * branch
    out = _dot(merged.astype(BF16), wo_ref[...])
    o_ref[...] = x + _rms_norm(out, post_g_ref[...])


def _merge(x2, pre_g, post_g, ys, w_gate, w_merge, w_branch, w_out):
    rows = x2.shape[0]
    row_spec = lambda w: pl.BlockSpec((ROW_TILE, w), lambda i: (i, 0))
    return pl.pallas_call(
        _merge_body,
        out_shape=jax.ShapeDtypeStruct((rows, D_MODEL), F32),
        grid=(rows // ROW_TILE,),
        in_specs=[row_spec(D_MODEL), _const_spec((1, D_MODEL)), _const_spec((1, D_MODEL)),
                  row_spec(WIDTH), row_spec(WIDTH), row_spec(WIDTH), row_spec(WIDTH),
                  _const_spec((D_MODEL, GATE_COLS)),
                  _const_spec((D_MODEL, N_BRANCH * D_MODEL)),
                  _const_spec((N_BRANCH, WIDTH, D_MODEL)),
                  _const_spec((D_MODEL, D_MODEL))],
        out_specs=row_spec(D_MODEL),
        compiler_params=_params("parallel"),
        name="merge",
    )(x2, pre_g, post_g, *ys, w_gate, w_merge, w_branch, w_out)


def _block_diag(blocks):
    n, r, c = blocks.shape
    eye = jnp.eye(n, dtype=blocks.dtype)
    return (blocks[:, :, None, :] * eye[:, None, :, None]).reshape(n * r, n * c)


def kernel(x, pre_norm_g, post_norm_g, w_in, ssm_a_re, ssm_a_im, ssm_log_dt, ssm_b_re, ssm_b_im, ssm_c_re,
           ssm_c_im, ssm_d, ssm_w_glu, ssm_b_glu, lru_conv_w, lru_conv_b, lru_w_a, lru_b_a, lru_w_x, lru_b_x,
           lru_lambda, w_branch, w_out):
    batch, seq, _ = x.shape
    depth = w_in.shape[0]
    g, n, c = SSM_GROUPS, SSM_STATE, SSM_GROUP
    assert batch % 16 == 0 and seq % max(ATT_TILE, RET_TILE, SCAN_STEPS) == 0
    assert (batch * seq) % ROW_TILE == 0

    tri = jnp.tril(jnp.ones((ATT_TILE, ATT_TILE), F32), -1).astype(BF16)
    ret_tabs = _retention_tables(seq)

    flat = lambda a: a.reshape(depth * g * n, -1)
    log_dt = jnp.broadcast_to(ssm_log_dt[:, :, None], (depth, g, n))
    lam_re, lam_im, bb_re, bb_im = _s5_prep(flat(ssm_a_re), flat(ssm_a_im), flat(log_dt),
                                             flat(ssm_b_re), flat(ssm_b_im))

    x_tm = jnp.transpose(x, (1, 0, 2)).reshape(seq * batch, D_MODEL)
    for l in range(depth):
        w_l = w_in[l].astype(BF16)
        bb = lambda a: _block_diag(jnp.transpose(a.reshape(depth, g, n, c)[l], (0, 2, 1)))
        lam_rows = lambda a: jnp.broadcast_to(a.reshape(depth, 1, g * n)[l], (batch, g * n))
        rec = {
            "bbd": jnp.concatenate([bb(bb_re), bb(bb_im)], axis=1).astype(BF16),
            "lam_re": lam_rows(lam_re), "lam_im": lam_rows(lam_im),
            "c_re": _block_diag(jnp.transpose(ssm_c_re[l], (0, 2, 1))).astype(BF16),
            "c_im": _block_diag(jnp.transpose(ssm_c_im[l], (0, 2, 1))).astype(BF16),
            "d_skip": ssm_d[l][None, :], "w_glu": ssm_w_glu[l].astype(BF16), "b_glu": ssm_b_glu[l][None, :],
            "conv_w": lru_conv_w[l], "conv_b": lru_conv_b[l][None, :],
            "w_a": _block_diag(lru_w_a[l]).astype(BF16), "b_a": lru_b_a[l].reshape(1, WIDTH),
            "w_x": _block_diag(lru_w_x[l]).astype(BF16), "b_x": lru_b_x[l].reshape(1, WIDTH),
            "lam": lru_lambda[l][None, :],
        }
        pre_g = pre_norm_g[l][None, :]
        mix_in = _inproj(x_tm, pre_g, w_l[:, :MIX_COLS])
        mix_sb = mix_in.reshape(seq, batch * MIX_COLS)
        y_sb = _stick_breaking(mix_sb, tri, seq, batch).reshape(seq * batch, WIDTH)
        y_ret = _retention(mix_sb, ret_tabs, seq, batch).reshape(seq * batch, WIDTH)
        y_ssm, y_lru = _recurrent(mix_in, rec, batch)
        x_tm = _merge(x_tm, pre_g, post_norm_g[l][None, :], (y_sb, y_ssm, y_ret, y_lru),
                      w_l[:, MIX_COLS:MIX_COLS + GATE_COLS], w_l[:, MIX_COLS + GATE_COLS:],
                      w_branch[l].astype(BF16), w_out[l].astype(BF16))
    return jnp.transpose(x_tm.reshape(seq, batch, D_MODEL), (1, 0, 2))
```

```python
import functools
import math

import jax
import jax.numpy as jnp
from jax import lax
from jax.experimental import pallas as pl
from jax.experimental.pallas import tpu as pltpu

F32 = jnp.float32
BF16 = jnp.bfloat16

D_MODEL = 1024
N_BRANCH = 4
WIDTH = D_MODEL // N_BRANCH
HEAD_DIM = 64
N_HEADS = WIDTH // HEAD_DIM
SSM_GROUPS = 16
SSM_GROUP = 16
SSM_STATE = 64
SSM_LANES = SSM_GROUPS * SSM_STATE
CONV_WIDTH = 4
LRU_C = 8.0
ROPE_BASE = 10000.0
EPS = 1e-6
LOG2_E = math.log2(math.e)
MIX_COLS = 8 * WIDTH
GATE_COLS = N_BRANCH * WIDTH

LANES = 128
SUBLANES = 8
ROW_TILE = 512
ATT_TILE = 256
ATT_BATCH = 2
RET_TILE = 256
SCAN_STEPS = 64
ROW_PITCH = SCAN_STEPS + SUBLANES
SCAN_SLABS = 4
VMEM_LIMIT = 56 * 1024 * 1024


def _dot(a, b):
    return jnp.dot(a, b, preferred_element_type=F32)


def _dot_nt(a, b):
    return lax.dot_general(a, b, (((1,), (1,)), ((), ())), preferred_element_type=F32)


def _sigmoid(x):
    return 0.5 * jnp.tanh(0.5 * x) + 0.5


def _rms_norm(x, g):
    return x * lax.rsqrt(jnp.mean(x * x, axis=-1, keepdims=True) + EPS) * g


def _const_spec(shape):
    nd = len(shape)
    return pl.BlockSpec(shape, lambda *_: (0,) * nd, pipeline_mode=pl.Buffered(1))


def _params(*sem):
    return pltpu.CompilerParams(dimension_semantics=sem, vmem_limit_bytes=VMEM_LIMIT)


def _inproj_body(x_ref, g_ref, w_ref, o_ref):
    h = _rms_norm(x_ref[...], g_ref[...]).astype(BF16)
    o_ref[...] = _dot(h, w_ref[...]).astype(o_ref.dtype)


def _inproj(x2, g, w_mix):
    rows = x2.shape[0]
    return pl.pallas_call(
        _inproj_body,
        out_shape=jax.ShapeDtypeStruct((rows, MIX_COLS), BF16),
        grid=(rows // ROW_TILE,),
        in_specs=[pl.BlockSpec((ROW_TILE, D_MODEL), lambda i: (i, 0)),
                  _const_spec((1, D_MODEL)),
                  _const_spec((D_MODEL, MIX_COLS))],
        out_specs=pl.BlockSpec((ROW_TILE, MIX_COLS), lambda i: (i, 0)),
        compiler_params=_params("parallel"),
        name="inproj",
    )(x2, g, w_mix)


def _sb_body(q_ref, k_ref, v_ref, tri_ref, o_ref, acc_ref):
    i = pl.program_id(1)
    nb, tq, _ = q_ref.shape
    lane_head = lax.broadcasted_iota(jnp.int32, (1, WIDTH), 1) // HEAD_DIM
    row = lax.broadcasted_iota(jnp.int32, (tq, tq), 0)
    col = lax.broadcasted_iota(jnp.int32, (tq, tq), 1)
    causal = col < row
    tri = tri_ref[...]
    acc_ref[...] = jnp.zeros_like(acc_ref)

    zero = jnp.zeros((), BF16)
    head_masks = [lane_head == h for h in range(N_HEADS)]
    z_scale = LOG2_E * HEAD_DIM ** -0.5
    q_heads = [[jnp.where(m, q_ref[n], zero) for m in head_masks] for n in range(nb)]

    def key_tile(j, carries, diagonal):
        k0 = pl.multiple_of(j * tq, tq)
        new_carries = []
        for n in range(nb):
            kj = k_ref[n, pl.ds(k0, tq), :]
            vj = v_ref[n, pl.ds(k0, tq), :]
            out = None
            for h in range(N_HEADS):
                carry = carries[n * N_HEADS + h]
                z = _dot_nt(q_heads[n][h], kj) * z_scale
                neg = jnp.minimum(z, 0.0)
                pos = z - neg
                soft = jnp.log2(1.0 + jnp.exp2(neg - pos))
                log_pass = neg - soft
                fail = pos + soft
                if diagonal:
                    fail = jnp.where(causal, fail, 0.0)
                hi = fail.astype(BF16)
                lo = (fail - hi.astype(F32)).astype(BF16)
                after = _dot(hi, tri) + _dot(lo, tri)
                w = jnp.exp2((log_pass - carry) - after)
                if diagonal:
                    w = jnp.where(causal, w, 0.0)
                part = _dot(w.astype(BF16), jnp.where(head_masks[h], vj, zero))
                out = part if out is None else out + part
                new_carries.append(carry + jnp.sum(fail, axis=-1, keepdims=True))
            acc_ref[n] += out
        return tuple(new_carries)

    carries = key_tile(i, (jnp.zeros((tq, 1), F32),) * (nb * N_HEADS), True)
    lax.fori_loop(0, i, lambda jj, c: key_tile(i - 1 - jj, c, False), carries)
    o_ref[...] = acc_ref[...].astype(o_ref.dtype)


def _stick_breaking(mix_in, tri):
    batch, seq, _ = mix_in.shape
    nb = ATT_BATCH
    return pl.pallas_call(
        _sb_body,
        out_shape=jax.ShapeDtypeStruct((batch, seq, WIDTH), BF16),
        grid=(batch // nb, seq // ATT_TILE),
        in_specs=[pl.BlockSpec((nb, ATT_TILE, WIDTH), lambda b, i: (b, i, 0)),
                  pl.BlockSpec((nb, seq, WIDTH), lambda b, i: (b, 0, 1)),
                  pl.BlockSpec((nb, seq, WIDTH), lambda b, i: (b, 0, 2)),
                  _const_spec((ATT_TILE, ATT_TILE))],
        out_specs=pl.BlockSpec((nb, ATT_TILE, WIDTH), lambda b, i: (b, i, 0)),
        scratch_shapes=[pltpu.VMEM((nb, ATT_TILE, WIDTH), F32)],
        compiler_params=_params("parallel", "arbitrary"),
        name="stick_breaking",
    )(mix_in, mix_in, mix_in, tri)


def _ret_body(q_ref, k_ref, v_ref, cos_ref, sin_ref, rot_ref, decay_ref, xi_ref, zeta_ref, cd_ref,
              bd_ref, avg_ref, o_ref, state_ref):
    @pl.when(pl.program_id(1) == 0)
    def _():
        state_ref[...] = jnp.zeros_like(state_ref)

    q = q_ref[...]
    k = k_ref[...]
    v = v_ref[...]
    cosf = cos_ref[...]
    sinf = sin_ref[...]
    rot = rot_ref[...]
    qr = q.astype(F32) * cosf + _dot(q, rot) * sinf
    kr = (k.astype(F32) * cosf + _dot(k, rot) * sinf) * (HEAD_DIM ** -0.5)
    qrb = qr.astype(BF16)
    krb = kr.astype(BF16)
    lane_head = lax.broadcasted_iota(jnp.int32, (1, WIDTH), 1) // HEAD_DIM
    zero = jnp.zeros((), BF16)
    o = None
    for h in range(N_HEADS):
        head_mask = lane_head == h
        s = _dot_nt(jnp.where(head_mask, qrb, zero), krb) * decay_ref[h]
        part = _dot(s.astype(BF16), jnp.where(head_mask, v, zero))
        o = part if o is None else o + part
    state = state_ref[...]
    o = o + _dot((qr * xi_ref[...]).astype(BF16), state.astype(BF16))
    kv = _dot((kr * zeta_ref[...]).T.astype(BF16), v)
    state_ref[...] = cd_ref[...] * state + kv * bd_ref[...]
    avg = avg_ref[...]
    d = o - _dot(o.astype(BF16), avg)
    var = _dot((d * d).astype(BF16), avg)
    o_ref[...] = (d * lax.rsqrt(var + EPS)).astype(o_ref.dtype)


def _retention(mix_in, tabs):
    batch, seq, _ = mix_in.shape
    c = RET_TILE
    tile = lambda col: pl.BlockSpec((None, c, WIDTH), lambda b, n: (b, n, col))
    return pl.pallas_call(
        _ret_body,
        out_shape=jax.ShapeDtypeStruct((batch, seq, WIDTH), BF16),
        grid=(batch, seq // c),
        in_specs=[tile(4), tile(5), tile(6),
                  pl.BlockSpec((c, WIDTH), lambda b, n: (n, 0)),
                  pl.BlockSpec((c, WIDTH), lambda b, n: (n, 0)),
                  _const_spec((WIDTH, WIDTH)),
                  _const_spec((N_HEADS, c, c)),
                  _const_spec((c, WIDTH)),
                  _const_spec((c, WIDTH)),
                  _const_spec((WIDTH, 1)),
                  _const_spec((WIDTH, WIDTH)),
                  _const_spec((WIDTH, WIDTH))],
        out_specs=pl.BlockSpec((None, c, WIDTH), lambda b, n: (b, n, 0)),
        scratch_shapes=[pltpu.VMEM((WIDTH, WIDTH), F32)],
        compiler_params=_params("parallel", "arbitrary"),
        name="retention",
    )(mix_in, mix_in, mix_in, tabs["cos"], tabs["sin"], tabs["rot"], tabs["decay"], tabs["xi"],
      tabs["zeta"], tabs["cd"], tabs["bd"], tabs["avg"])


def _retention_tables(seq):
    c = RET_TILE
    half = HEAD_DIM // 2
    inv_freq = ROPE_BASE ** (-jnp.arange(half, dtype=F32) / half)
    ang = jnp.arange(seq, dtype=F32)[:, None] * inv_freq[None, :]
    cos = jnp.tile(jnp.cos(ang), (1, 2 * N_HEADS))
    sin = jnp.tile(jnp.sin(ang), (1, 2 * N_HEADS))
    lane = jnp.arange(WIDTH)
    d = lane % HEAD_DIM
    src = jnp.where(d < half, lane + half, lane - half)
    sign = jnp.where(d < half, -1.0, 1.0)
    rot = (jnp.zeros((WIDTH, WIDTH), F32).at[src, lane].set(sign)).astype(BF16)
    log_g = jnp.log1p(-(2.0 ** (-5.0 - jnp.arange(N_HEADS, dtype=F32))))
    i = jnp.arange(c, dtype=F32)
    rel = i[:, None] - i[None, :]
    decay = jnp.where(rel >= 0, jnp.exp(log_g[:, None, None] * jnp.maximum(rel, 0.0)), 0.0)
    log_g_lane = log_g[lane // HEAD_DIM]
    xi = jnp.exp(log_g_lane[None, :] * (i + 1.0)[:, None])
    zeta = jnp.exp(log_g_lane[None, :] * (c - 1 - i)[:, None])
    cd = jnp.exp(log_g_lane * c)[:, None]
    same_head = (lane[:, None] // HEAD_DIM) == (lane[None, :] // HEAD_DIM)
    return {"cos": cos, "sin": sin, "rot": rot, "decay": decay, "xi": xi, "zeta": zeta, "cd": cd,
            "bd": same_head.astype(F32), "avg": (same_head.astype(F32) / HEAD_DIM).astype(BF16)}


def _s5_prep_body(a_re_ref, a_im_ref, log_dt_ref, b_re_ref, b_im_ref, lam_re_ref, lam_im_ref,
                  bb_re_ref, bb_im_ref):
    a_re = a_re_ref[...]
    a_im = a_im_ref[...]
    dt = jnp.exp(log_dt_ref[...])
    mag = jnp.exp(dt * a_re)
    ab_re = mag * jnp.cos(dt * a_im)
    ab_im = mag * jnp.sin(dt * a_im)
    den = a_re * a_re + a_im * a_im
    num_re = ab_re - 1.0
    f_re = (num_re * a_re + ab_im * a_im) / den
    f_im = (ab_im * a_re - num_re * a_im) / den
    b_re = b_re_ref[...]
    b_im = b_im_ref[...]
    lam_re_ref[...] = ab_re
    lam_im_ref[...] = ab_im
    bb_re_ref[...] = f_re * b_re - f_im * b_im
    bb_im_ref[...] = f_re * b_im + f_im * b_re


def _s5_prep(a_re, a_im, log_dt, b_re, b_im):
    rows = a_re.shape[0]
    col = jax.ShapeDtypeStruct((rows, 1), F32)
    mat = jax.ShapeDtypeStruct((rows, SSM_GROUP), F32)
    return pl.pallas_call(_s5_prep_body, out_shape=(col, col, mat, mat), name="s5_prep")(
        a_re, a_im, log_dt, b_re, b_im)


def _gelu_tanh(y):
    return 0.5 * y * (1.0 + jnp.tanh(math.sqrt(2.0 / math.pi) * (y + 0.044715 * (y * y * y))))


def _to_slabs(slab_ref, first, val, batch, steps):
    for j in range(val.shape[1] // LANES):
        for b in range(batch):
            slab_ref[first + j, b * ROW_PITCH:b * ROW_PITCH + steps, :] = (
                val[b * steps:(b + 1) * steps, j * LANES:(j + 1) * LANES])


def _from_slabs(slab_ref, first, count, batch, steps):
    rows = [jnp.concatenate([slab_ref[first + j, b * ROW_PITCH:b * ROW_PITCH + steps, :] for j in range(count)],
                            axis=-1) for b in range(batch)]
    return jnp.concatenate(rows, axis=0)


def _scan_body(us_ref, ul_ref, bbd_ref, lam_re_ref, lam_im_ref, c_re_ref, c_im_ref, dskip_ref, wglu_ref,
               bglu_ref, convw_ref, convb_ref, wa_ref, ba_ref, wx_ref, bx_ref, lam_lru_ref,
               ys_ref, yl_ref,
               x_scr, h_scr, hstate, ext_scr, ab_scr, yl_scr, lstate):
    batch, steps, _ = us_ref.shape
    rows = batch * steps
    n_slabs = SSM_LANES // LANES
    lru_slabs = WIDTH // LANES
    head = SUBLANES
    step_rows = lambda t: pl.ds(t, batch, stride=ROW_PITCH)

    @pl.when(pl.program_id(0) == 0)
    def _():
        hstate[...] = jnp.zeros_like(hstate)
        lstate[...] = jnp.zeros_like(lstate)
        ext_scr[:, 0:head, :] = jnp.zeros((batch, head, WIDTH), F32)
        h_scr[...] = jnp.zeros_like(h_scr)

    us = us_ref[...].reshape(rows, WIDTH)
    for jj in range(2 * n_slabs // 2):
        _to_slabs(x_scr, 2 * jj, _dot(us, bbd_ref[:, 2 * jj * LANES:(2 * jj + 2) * LANES]), batch, steps)
    for c0 in range(0, n_slabs, SCAN_SLABS):
        lam = [(lam_re_ref[:, (c0 + s) * LANES:(c0 + s + 1) * LANES],
                lam_im_ref[:, (c0 + s) * LANES:(c0 + s + 1) * LANES]) for s in range(SCAN_SLABS)]

        def step(t, carry, c0=c0, lam=lam):
            new = []
            for s in range(SCAN_SLABS):
                h_re, h_im = carry[2 * s], carry[2 * s + 1]
                lam_re, lam_im = lam[s]
                n_re = lam_re * h_re - lam_im * h_im + x_scr.at[c0 + s][step_rows(t), :]
                n_im = lam_re * h_im + lam_im * h_re + x_scr.at[n_slabs + c0 + s][step_rows(t), :]
                h_scr.at[c0 + s][step_rows(t), :] = n_re
                h_scr.at[n_slabs + c0 + s][step_rows(t), :] = n_im
                new += [n_re, n_im]
            return tuple(new)

        init = []
        for s in range(SCAN_SLABS):
            init += [hstate[c0 + s], hstate[n_slabs + c0 + s]]
        final = lax.fori_loop(0, steps, step, tuple(init), unroll=2)
        for s in range(SCAN_SLABS):
            hstate[c0 + s] = final[2 * s]
            hstate[n_slabs + c0 + s] = final[2 * s + 1]
    y_pad = None
    for jj in range(n_slabs // 2):
        h_re = jnp.concatenate([h_scr[2 * jj], h_scr[2 * jj + 1]], axis=-1).astype(BF16)
        h_im = jnp.concatenate([h_scr[n_slabs + 2 * jj], h_scr[n_slabs + 2 * jj + 1]], axis=-1).astype(BF16)
        part = (_dot(h_re, c_re_ref[2 * jj * LANES:(2 * jj + 2) * LANES, :])
                - _dot(h_im, c_im_ref[2 * jj * LANES:(2 * jj + 2) * LANES, :]))
        y_pad = part if y_pad is None else y_pad + part
    y = jnp.concatenate([y_pad[b * ROW_PITCH:b * ROW_PITCH + steps] for b in range(batch)], axis=0)
    y = _gelu_tanh(y + dskip_ref[...] * us.astype(F32))
    ys = y * _sigmoid(_dot(y.astype(BF16), wglu_ref[...]) + bglu_ref[...])
    ys_ref[...] = ys.reshape(batch, steps, WIDTH).astype(ys_ref.dtype)

    ext_scr[:, head:head + steps, :] = ul_ref[...].astype(F32)
    xc = convb_ref[...]
    for kk in range(CONV_WIDTH):
        lo = head - (CONV_WIDTH - 1) + kk
        xc = xc + convw_ref[kk:kk + 1, :] * ext_scr[:, lo:lo + steps, :]
    ext_scr[:, 0:head, :] = ext_scr[:, steps:steps + head, :]
    xc = xc.reshape(rows, WIDTH)
    xcb = xc.astype(BF16)
    r = _sigmoid(_dot(xcb, wa_ref[...]) + ba_ref[...])
    ig = _sigmoid(_dot(xcb, wx_ref[...]) + bx_ref[...])
    neg_lam = -lam_lru_ref[...]
    softplus = jnp.maximum(neg_lam, 0.0) + jnp.log(1.0 + jnp.exp(-jnp.abs(neg_lam)))
    a = jnp.exp(-LRU_C * r * softplus)
    _to_slabs(ab_scr, 0, a, batch, steps)
    _to_slabs(ab_scr, lru_slabs, jnp.sqrt(1.0 - a * a) * (ig * xc), batch, steps)

    def lru_step(t, carry):
        new = []
        for s in range(lru_slabs):
            h = ab_scr.at[s][step_rows(t), :] * carry[s] + ab_scr.at[lru_slabs + s][step_rows(t), :]
            yl_scr.at[s][step_rows(t), :] = h
            new.append(h)
        return tuple(new)

    final = lax.fori_loop(0, steps, lru_step, tuple(lstate[s] for s in range(lru_slabs)), unroll=4)
    for s in range(lru_slabs):
        lstate[s] = final[s]
    yl_ref[...] = _from_slabs(yl_scr, 0, lru_slabs, batch, steps).reshape(batch, steps, WIDTH).astype(yl_ref.dtype)


def _recurrent(mix_in, p):
    batch, seq, _ = mix_in.shape
    steps = SCAN_STEPS
    slab_rows = batch * ROW_PITCH
    vec = lambda: _const_spec((1, WIDTH))
    sq = lambda: _const_spec((WIDTH, WIDTH))
    out = jax.ShapeDtypeStruct((batch, seq, WIDTH), BF16)
    return pl.pallas_call(
        _scan_body,
        out_shape=(out, out),
        grid=(seq // steps,),
        in_specs=[pl.BlockSpec((batch, steps, WIDTH), lambda t: (0, t, 3)),
                  pl.BlockSpec((batch, steps, WIDTH), lambda t: (0, t, 7)),
                  _const_spec((WIDTH, 2 * SSM_LANES)),
                  _const_spec((batch, SSM_LANES)), _const_spec((batch, SSM_LANES)),
                  _const_spec((SSM_LANES, WIDTH)), _const_spec((SSM_LANES, WIDTH)),
                  vec(), sq(), vec(),
                  _const_spec((CONV_WIDTH, WIDTH)), vec(), sq(), vec(), sq(), vec(), vec()],
        out_specs=(pl.BlockSpec((batch, steps, WIDTH), lambda t: (0, t, 0)),
                   pl.BlockSpec((batch, steps, WIDTH), lambda t: (0, t, 0))),
        scratch_shapes=[pltpu.VMEM((2 * SSM_LANES // LANES, slab_rows, LANES), F32),
                        pltpu.VMEM((2 * SSM_LANES // LANES, slab_rows, LANES), F32),
                        pltpu.VMEM((2 * SSM_LANES // LANES, batch, LANES), F32),
                        pltpu.VMEM((batch, steps + SUBLANES, WIDTH), F32),
                        pltpu.VMEM((2 * WIDTH // LANES, slab_rows, LANES), F32),
                        pltpu.VMEM((WIDTH // LANES, slab_rows, LANES), F32),
                        pltpu.VMEM((WIDTH // LANES, batch, LANES), F32)],
        compiler_params=_params("arbitrary"),
        name="recurrent",
    )(mix_in, mix_in, p["bbd"], p["lam_re"], p["lam_im"], p["c_re"], p["c_im"], p["d_skip"], p["w_glu"],
      p["b_glu"], p["conv_w"], p["conv_b"], p["w_a"], p["b_a"], p["w_x"], p["b_x"], p["lam"])


def _merge_body(x_ref, pre_g_ref, post_g_ref, y0_ref, y1_ref, y2_ref, y3_ref, wg_ref, wm_ref, wb_ref,
                wo_ref, o_ref):
    x = x_ref[...]
    h = _rms_norm(x, pre_g_ref[...]).astype(BF16)
    merged = None
    for n, y_ref in enumerate((y0_ref, y1_ref, y2_ref, y3_ref)):
        g = _dot(h, wg_ref[:, n * WIDTH:(n + 1) * WIDTH])
        mix = y_ref[...].astype(F32) * (g * _sigmoid(g))
        branch = _dot(mix.astype(BF16), wb_ref[n])
        gate = _sigmoid(_dot(h, wm_ref[:, n * D_MODEL:(n + 1) * D_MODEL]))
        merged = gate * branch if merged is None else merged + gate * branch
    out = _dot(merged.astype(BF16), wo_ref[...])
    o_ref[...] = x + _rms_norm(out, post_g_ref[...])


def _merge(x2, pre_g, post_g, ys, w_gate, w_merge, w_branch, w_out):
    rows = x2.shape[0]
    row_spec = lambda w: pl.BlockSpec((ROW_TILE, w), lambda i: (i, 0))
    return pl.pallas_call(
        _merge_body,
        out_shape=jax.ShapeDtypeStruct((rows, D_MODEL), F32),
        grid=(rows // ROW_TILE,),
        in_specs=[row_spec(D_MODEL), _const_spec((1, D_MODEL)), _const_spec((1, D_MODEL)),
                  row_spec(WIDTH), row_spec(WIDTH), row_spec(WIDTH), row_spec(WIDTH),
                  _const_spec((D_MODEL, GATE_COLS)),
                  _const_spec((D_MODEL, N_BRANCH * D_MODEL)),
                  _const_spec((N_BRANCH, WIDTH, D_MODEL)),
                  _const_spec((D_MODEL, D_MODEL))],
        out_specs=row_spec(D_MODEL),
        compiler_params=_params("parallel"),
        name="merge",
    )(x2, pre_g, post_g, *ys, w_gate, w_merge, w_branch, w_out)


def _block_diag(blocks):
    n, r, c = blocks.shape
    eye = jnp.eye(n, dtype=blocks.dtype)
    return (blocks[:, :, None, :] * eye[:, None, :, None]).reshape(n * r, n * c)


def kernel(x, pre_norm_g, post_norm_g, w_in, ssm_a_re, ssm_a_im, ssm_log_dt, ssm_b_re, ssm_b_im, ssm_c_re,
           ssm_c_im, ssm_d, ssm_w_glu, ssm_b_glu, lru_conv_w, lru_conv_b, lru_w_a, lru_b_a, lru_w_x, lru_b_x,
           lru_lambda, w_branch, w_out):
    batch, seq, _ = x.shape
    depth = w_in.shape[0]
    g, n, c = SSM_GROUPS, SSM_STATE, SSM_GROUP
    assert batch % (2 * SUBLANES) == 0 and batch % ATT_BATCH == 0
    assert seq % max(ATT_TILE, RET_TILE, SCAN_STEPS, ROW_TILE) == 0

    tri = jnp.tril(jnp.ones((ATT_TILE, ATT_TILE), F32), -1).astype(BF16)
    ret_tabs = _retention_tables(seq)

    flat = lambda a: a.reshape(depth * g * n, -1)
    log_dt = jnp.broadcast_to(ssm_log_dt[:, :, None], (depth, g, n))
    lam_re, lam_im, bb_re, bb_im = _s5_prep(flat(ssm_a_re), flat(ssm_a_im), flat(log_dt),
                                             flat(ssm_b_re), flat(ssm_b_im))

    rows = batch * seq
    x2 = x.reshape(rows, D_MODEL)
    for l in range(depth):
        w_l = w_in[l].astype(BF16)
        bb = lambda a: _block_diag(jnp.transpose(a.reshape(depth, g, n, c)[l], (0, 2, 1)))
        lam_rows = lambda a: jnp.broadcast_to(a.reshape(depth, 1, g * n)[l], (batch, g * n))
        rec = {
            "bbd": jnp.concatenate([bb(bb_re), bb(bb_im)], axis=1).astype(BF16),
            "lam_re": lam_rows(lam_re), "lam_im": lam_rows(lam_im),
            "c_re": _block_diag(jnp.transpose(ssm_c_re[l], (0, 2, 1))).astype(BF16),
            "c_im": _block_diag(jnp.transpose(ssm_c_im[l], (0, 2, 1))).astype(BF16),
            "d_skip": ssm_d[l][None, :], "w_glu": ssm_w_glu[l].astype(BF16), "b_glu": ssm_b_glu[l][None, :],
            "conv_w": lru_conv_w[l], "conv_b": lru_conv_b[l][None, :],
            "w_a": _block_diag(lru_w_a[l]).astype(BF16), "b_a": lru_b_a[l].reshape(1, WIDTH),
            "w_x": _block_diag(lru_w_x[l]).astype(BF16), "b_x": lru_b_x[l].reshape(1, WIDTH),
            "lam": lru_lambda[l][None, :],
        }
        pre_g = pre_norm_g[l][None, :]
        mix_in = _inproj(x2, pre_g, w_l[:, :MIX_COLS]).reshape(batch, seq, MIX_COLS)
        y_sb = _stick_breaking(mix_in, tri)
        y_ret = _retention(mix_in, ret_tabs)
        y_ssm, y_lru = _recurrent(mix_in, rec)
        ys = tuple(y.reshape(rows, WIDTH) for y in (y_sb, y_ssm, y_ret, y_lru))
        x2 = _merge(x2, pre_g, post_norm_g[l][None, :], ys,
                    w_l[:, MIX_COLS:MIX_COLS + GATE_COLS], w_l[:, MIX_COLS + GATE_COLS:],
                    w_branch[l].astype(BF16), w_out[l].astype(BF16))
    return x2.reshape(batch, seq, D_MODEL)
```

```python
import functools
import math

import jax
import jax.numpy as jnp
from jax import lax
from jax.experimental import pallas as pl
from jax.experimental.pallas import tpu as pltpu

F32 = jnp.float32
BF16 = jnp.bfloat16

D_MODEL = 1024
N_BRANCH = 4
WIDTH = D_MODEL // N_BRANCH
HEAD_DIM = 64
N_HEADS = WIDTH // HEAD_DIM
SSM_GROUPS = 16
SSM_GROUP = 16
SSM_STATE = 64
SSM_LANES = SSM_GROUPS * SSM_STATE
CONV_WIDTH = 4
LRU_C = 8.0
ROPE_BASE = 10000.0
EPS = 1e-6
LOG2_E = math.log2(math.e)
TINY = 1e-30
MIX_COLS = 8 * WIDTH
GATE_COLS = N_BRANCH * WIDTH

LANES = 128
SUBLANES = 8
ROW_TILE = 512
ATT_TILE = 256
ATT_BATCH = 4
RET_TILE = 256
RET_BATCH = 2
SCAN_STEPS = 64
SEQ_PAD = 4
SCAN_SLABS = 4
VMEM_LIMIT = 56 * 1024 * 1024


def _dot(a, b):
    return jnp.dot(a, b, preferred_element_type=F32)


def _dot_nt(a, b):
    return lax.dot_general(a, b, (((1,), (1,)), ((), ())), preferred_element_type=F32)


def _sigmoid(x):
    return 0.5 * jnp.tanh(0.5 * x) + 0.5


def _rms_norm(x, g):
    return x * lax.rsqrt(jnp.mean(x * x, axis=-1, keepdims=True) + EPS) * g


def _const_spec(shape):
    nd = len(shape)
    return pl.BlockSpec(shape, lambda *_: (0,) * nd, pipeline_mode=pl.Buffered(1))


def _params(*sem):
    return pltpu.CompilerParams(dimension_semantics=sem, vmem_limit_bytes=VMEM_LIMIT)


def _inproj_body(x_ref, g_ref, w_ref, o_ref):
    h = _rms_norm(x_ref[...], g_ref[...]).astype(BF16)
    o_ref[...] = _dot(h, w_ref[...]).astype(o_ref.dtype)


def _inproj(x2, g, w_mix):
    rows = x2.shape[0]
    return pl.pallas_call(
        _inproj_body,
        out_shape=jax.ShapeDtypeStruct((rows, MIX_COLS), BF16),
        grid=(rows // ROW_TILE,),
        in_specs=[pl.BlockSpec((ROW_TILE, D_MODEL), lambda i: (i, 0)),
                  _const_spec((1, D_MODEL)),
                  _const_spec((D_MODEL, MIX_COLS))],
        out_specs=pl.BlockSpec((ROW_TILE, MIX_COLS), lambda i: (i, 0)),
        compiler_params=_params("parallel"),
        name="inproj",
    )(x2, g, w_mix)


def _sb_body(q_ref, k_ref, v_ref, tri_ref, o_ref, acc_ref):
    i = pl.program_id(1)
    nb, tq, _ = q_ref.shape
    lane_head = lax.broadcasted_iota(jnp.int32, (1, WIDTH), 1) // HEAD_DIM
    row = lax.broadcasted_iota(jnp.int32, (tq, tq), 0)
    col = lax.broadcasted_iota(jnp.int32, (tq, tq), 1)
    causal = col < row
    tri = tri_ref[...]
    acc_ref[...] = jnp.zeros_like(acc_ref)

    zero = jnp.zeros((), BF16)
    head_masks = [lane_head == h for h in range(N_HEADS)]
    z_scale = LOG2_E * HEAD_DIM ** -0.5
    q_heads = [[jnp.where(m, q_ref[n], zero) for m in head_masks] for n in range(nb)]

    def key_tile(j, carries, diagonal):
        k0 = pl.multiple_of(j * tq, tq)
        chains = [(n, h) for n in range(nb) for h in range(N_HEADS)]
        kj = [k_ref[n, pl.ds(k0, tq), :] for n in range(nb)]
        vj = [v_ref[n, pl.ds(k0, tq), :] for n in range(nb)]
        nc = len(chains)
        scores, fails, biases, afters, parts = {}, {}, {}, {}, {}
        for slot in range(nc + 2):
            if slot < nc:
                n, h = chains[slot]
                scores[slot] = _dot_nt(q_heads[n][h], kj[n])
            c = slot - 1
            if 0 <= c < nc:
                z = scores.pop(c) * z_scale
                neg = jnp.minimum(z, 0.0)
                pos = z - neg
                soft = jnp.log2(1.0 + jnp.exp2(neg - pos))
                fail = pos + soft
                if diagonal:
                    fail = jnp.where(causal, fail, 0.0)
                fails[c] = fail.astype(BF16)
                biases[c] = (neg - soft) - carries[c]
                afters[c] = _dot(fails[c], tri)
            c = slot - 2
            if 0 <= c < nc:
                n, h = chains[c]
                w = jnp.exp2(biases.pop(c) - afters[c])
                if diagonal:
                    w = jnp.where(causal, w, 0.0)
                parts[c] = _dot(w.astype(BF16), jnp.where(head_masks[h], vj[n], zero))
        for n in range(nb):
            out = parts[n * N_HEADS]
            for h in range(1, N_HEADS):
                out = out + parts[n * N_HEADS + h]
            acc_ref[n] += out
        return tuple(carry + (afters[c][:, 0:1] + fails[c][:, 0:1].astype(F32)) for c, carry in enumerate(carries))

    carries = key_tile(i, (jnp.zeros((tq, 1), F32),) * (nb * N_HEADS), True)
    lax.fori_loop(0, i, lambda jj, c: key_tile(i - 1 - jj, c, False), carries)
    o_ref[...] = acc_ref[...].astype(o_ref.dtype)


def _stick_breaking(mix_in, tri):
    batch, seq, _ = mix_in.shape
    nb = ATT_BATCH
    return pl.pallas_call(
        _sb_body,
        out_shape=jax.ShapeDtypeStruct((batch, seq, WIDTH), BF16),
        grid=(batch // nb, seq // ATT_TILE),
        in_specs=[pl.BlockSpec((nb, ATT_TILE, WIDTH), lambda b, i: (b, i, 0)),
                  pl.BlockSpec((nb, seq, WIDTH), lambda b, i: (b, 0, 1)),
                  pl.BlockSpec((nb, seq, WIDTH), lambda b, i: (b, 0, 2)),
                  _const_spec((ATT_TILE, ATT_TILE))],
        out_specs=pl.BlockSpec((nb, ATT_TILE, WIDTH), lambda b, i: (b, i, 0)),
        scratch_shapes=[pltpu.VMEM((nb, ATT_TILE, WIDTH), F32)],
        compiler_params=_params("parallel", "arbitrary"),
        name="stick_breaking",
    )(mix_in, mix_in, mix_in, tri)


def _ret_body(q_ref, k_ref, v_ref, cos_ref, sin_ref, rot_ref, decay_ref, xi_ref, zeta_ref, cd_ref,
              bd_ref, avg_ref, o_ref, state_ref):
    @pl.when(pl.program_id(1) == 0)
    def _():
        state_ref[...] = jnp.zeros_like(state_ref)

    cosf = cos_ref[...]
    sinf = sin_ref[...]
    rot = rot_ref[...]
    avg = avg_ref[...]
    lane_head = lax.broadcasted_iota(jnp.int32, (1, WIDTH), 1) // HEAD_DIM
    zero = jnp.zeros((), BF16)
    for n in range(q_ref.shape[0]):
        q = q_ref[n]
        k = k_ref[n]
        v = v_ref[n]
        qr = q.astype(F32) * cosf + _dot(q, rot) * sinf
        kr = (k.astype(F32) * cosf + _dot(k, rot) * sinf) * (HEAD_DIM ** -0.5)
        qrb = qr.astype(BF16)
        krb = kr.astype(BF16)
        o = None
        for h in range(N_HEADS):
            head_mask = lane_head == h
            s = _dot_nt(jnp.where(head_mask, qrb, zero), krb) * decay_ref[h]
            part = _dot(s.astype(BF16), jnp.where(head_mask, v, zero))
            o = part if o is None else o + part
        state = state_ref[n]
        o = o + _dot((qr * xi_ref[...]).astype(BF16), state.astype(BF16))
        kv = _dot((kr * zeta_ref[...]).T.astype(BF16), v)
        state_ref[n] = cd_ref[...] * state + kv * bd_ref[...]
        d = o - _dot(o.astype(BF16), avg)
        var = _dot((d * d).astype(BF16), avg)
        o_ref[n] = (d * lax.rsqrt(var + EPS)).astype(o_ref.dtype)


def _retention(mix_in, tabs):
    batch, seq, _ = mix_in.shape
    c = RET_TILE
    nb = RET_BATCH
    tile = lambda col: pl.BlockSpec((nb, c, WIDTH), lambda b, n: (b, n, col))
    return pl.pallas_call(
        _ret_body,
        out_shape=jax.ShapeDtypeStruct((batch, seq, WIDTH), BF16),
        grid=(batch // nb, seq // c),
        in_specs=[tile(4), tile(5), tile(6),
                  pl.BlockSpec((c, WIDTH), lambda b, n: (n, 0)),
                  pl.BlockSpec((c, WIDTH), lambda b, n: (n, 0)),
                  _const_spec((WIDTH, WIDTH)),
                  _const_spec((N_HEADS, c, c)),
                  _const_spec((c, WIDTH)),
                  _const_spec((c, WIDTH)),
                  _const_spec((WIDTH, 1)),
                  _const_spec((WIDTH, WIDTH)),
                  _const_spec((WIDTH, WIDTH))],
        out_specs=pl.BlockSpec((nb, c, WIDTH), lambda b, n: (b, n, 0)),
        scratch_shapes=[pltpu.VMEM((nb, WIDTH, WIDTH), F32)],
        compiler_params=_params("parallel", "arbitrary"),
        name="retention",
    )(mix_in, mix_in, mix_in, tabs["cos"], tabs["sin"], tabs["rot"], tabs["decay"], tabs["xi"],
      tabs["zeta"], tabs["cd"], tabs["bd"], tabs["avg"])


def _retention_tables(seq):
    c = RET_TILE
    half = HEAD_DIM // 2
    inv_freq = ROPE_BASE ** (-jnp.arange(half, dtype=F32) / half)
    ang = jnp.arange(seq, dtype=F32)[:, None] * inv_freq[None, :]
    cos = jnp.tile(jnp.cos(ang), (1, 2 * N_HEADS))
    sin = jnp.tile(jnp.sin(ang), (1, 2 * N_HEADS))
    lane = jnp.arange(WIDTH)
    d = lane % HEAD_DIM
    src = jnp.where(d < half, lane + half, lane - half)
    sign = jnp.where(d < half, -1.0, 1.0)
    rot = (jnp.zeros((WIDTH, WIDTH), F32).at[src, lane].set(sign)).astype(BF16)
    log_g = jnp.log1p(-(2.0 ** (-5.0 - jnp.arange(N_HEADS, dtype=F32))))
    i = jnp.arange(c, dtype=F32)
    rel = i[:, None] - i[None, :]
    decay = jnp.where(rel >= 0, jnp.exp(log_g[:, None, None] * jnp.maximum(rel, 0.0)), 0.0)
    log_g_lane = log_g[lane // HEAD_DIM]
    xi = jnp.exp(log_g_lane[None, :] * (i + 1.0)[:, None])
    zeta = jnp.exp(log_g_lane[None, :] * (c - 1 - i)[:, None])
    cd = jnp.exp(log_g_lane * c)[:, None]
    same_head = (lane[:, None] // HEAD_DIM) == (lane[None, :] // HEAD_DIM)
    return {"cos": cos, "sin": sin, "rot": rot, "decay": decay, "xi": xi, "zeta": zeta, "cd": cd,
            "bd": same_head.astype(F32), "avg": (same_head.astype(F32) / HEAD_DIM).astype(BF16)}


def _s5_prep_body(a_re_ref, a_im_ref, log_dt_ref, b_re_ref, b_im_ref, lam_re_ref, lam_im_ref,
                  bb_re_ref, bb_im_ref):
    a_re = a_re_ref[...]
    a_im = a_im_ref[...]
    dt = jnp.exp(log_dt_ref[...])
    mag = jnp.exp(dt * a_re)
    ab_re = mag * jnp.cos(dt * a_im)
    ab_im = mag * jnp.sin(dt * a_im)
    den = a_re * a_re + a_im * a_im
    num_re = ab_re - 1.0
    f_re = (num_re * a_re + ab_im * a_im) / den
    f_im = (ab_im * a_re - num_re * a_im) / den
    b_re = b_re_ref[...]
    b_im = b_im_ref[...]
    lam_re_ref[...] = ab_re
    lam_im_ref[...] = ab_im
    bb_re_ref[...] = f_re * b_re - f_im * b_im
    bb_im_ref[...] = f_re * b_im + f_im * b_re


def _s5_prep(a_re, a_im, log_dt, b_re, b_im):
    rows = a_re.shape[0]
    col = jax.ShapeDtypeStruct((rows, 1), F32)
    mat = jax.ShapeDtypeStruct((rows, SSM_GROUP), F32)
    return pl.pallas_call(_s5_prep_body, out_shape=(col, col, mat, mat), name="s5_prep")(
        a_re, a_im, log_dt, b_re, b_im)


def _gelu_tanh(y):
    return 0.5 * y * (1.0 + jnp.tanh(math.sqrt(2.0 / math.pi) * (y + 0.044715 * (y * y * y))))


def _scan_body(us_ref, ul_ref, bbd_ref, lam_re_ref, lam_im_ref, c_re_ref, c_im_ref, dskip_ref, wglu_ref,
               bglu_ref, convw_ref, convb_ref, wa_ref, ba_ref, wx_ref, bx_ref, lam_lru_ref,
               ys_ref, yl_ref,
               x_scr, h_scr, hstate, us_scr, ul_scr, ab_scr, yl_scr, out_scr, lstate):
    batch, steps, _ = us_ref.shape
    n_slabs = SSM_LANES // LANES
    lru_slabs = WIDTH // LANES
    pitch = steps + SEQ_PAD
    slab_rows = x_scr.shape[1]
    lead = SUBLANES
    step_rows = lambda t: pl.ds(SEQ_PAD + t, batch, stride=pitch)
    seq_rows = lambda b: slice(b * pitch + SEQ_PAD, (b + 1) * pitch)

    @pl.when(pl.program_id(0) == 0)
    def _():
        hstate[...] = jnp.zeros_like(hstate)
        lstate[...] = jnp.zeros_like(lstate)
        us_scr[...] = jnp.zeros_like(us_scr)
        ul_scr[...] = jnp.zeros_like(ul_scr)
        h_scr[...] = jnp.zeros_like(h_scr)
        yl_scr[...] = jnp.zeros_like(yl_scr)

    for b in range(batch):
        us_scr[seq_rows(b), :] = us_ref[b].astype(F32)
        ul_scr[lead + b * pitch + SEQ_PAD:lead + (b + 1) * pitch, :] = ul_ref[b].astype(F32)
    us = us_scr[...]
    usb = us.astype(BF16)
    half = slab_rows // 2
    chunk = slab_rows // n_slabs
    neg_lam = -lam_lru_ref[...]
    softplus = jnp.maximum(neg_lam, 0.0) + jnp.log(1.0 + jnp.exp(-jnp.abs(neg_lam)))
    for jj in range(n_slabs):
        for r0 in (0, half):
            part = _dot(usb[r0:r0 + half], bbd_ref[:, 2 * jj * LANES:(2 * jj + 2) * LANES])
            for j in range(2):
                x_scr[2 * jj + j, r0:r0 + half, :] = part[:, j * LANES:(j + 1) * LANES]
        r0 = jj * chunk
        xc = convb_ref[...]
        for kk in range(CONV_WIDTH):
            lo = lead - (CONV_WIDTH - 1) + kk + r0
            xc = xc + convw_ref[kk:kk + 1, :] * ul_scr[lo:lo + chunk, :]
        xcb = xc.astype(BF16)
        r = _sigmoid(_dot(xcb, wa_ref[...]) + ba_ref[...])
        ig = _sigmoid(_dot(xcb, wx_ref[...]) + bx_ref[...])
        a = jnp.exp(-LRU_C * r * softplus)
        gap = 1.0 - a * a
        bterm = gap * lax.rsqrt(jnp.maximum(gap, TINY)) * (ig * xc)
        for j in range(lru_slabs):
            ab_scr[j, r0:r0 + chunk, :] = a[:, j * LANES:(j + 1) * LANES]
            ab_scr[lru_slabs + j, r0:r0 + chunk, :] = bterm[:, j * LANES:(j + 1) * LANES]
    for b in range(batch):
        ul_scr[lead + b * pitch:lead + b * pitch + SEQ_PAD, :] = (
            ul_scr[lead + (b + 1) * pitch - SEQ_PAD:lead + (b + 1) * pitch, :])

    for c0 in range(0, n_slabs, SCAN_SLABS):
        lam = [(lam_re_ref[:, (c0 + s) * LANES:(c0 + s + 1) * LANES],
                lam_im_ref[:, (c0 + s) * LANES:(c0 + s + 1) * LANES]) for s in range(SCAN_SLABS)]
        with_lru = c0 == 0

        def step(t, carry, c0=c0, lam=lam, with_lru=with_lru):
            new = []
            for s in range(SCAN_SLABS):
                h_re, h_im = carry[2 * s], carry[2 * s + 1]
                lam_re, lam_im = lam[s]
                n_re = lam_re * h_re - lam_im * h_im + x_scr.at[c0 + s][step_rows(t), :]
                n_im = lam_re * h_im + lam_im * h_re + x_scr.at[n_slabs + c0 + s][step_rows(t), :]
                h_scr.at[c0 + s][step_rows(t), :] = n_re
                h_scr.at[n_slabs + c0 + s][step_rows(t), :] = n_im
                new += [n_re, n_im]
            if with_lru:
                for s in range(lru_slabs):
                    h = (ab_scr.at[s][step_rows(t), :] * carry[2 * SCAN_SLABS + s]
                         + ab_scr.at[lru_slabs + s][step_rows(t), :])
                    yl_scr.at[s][step_rows(t), :] = h
                    new.append(h)
            return tuple(new)

        init = []
        for s in range(SCAN_SLABS):
            init += [hstate[c0 + s], hstate[n_slabs + c0 + s]]
        if with_lru:
            init += [lstate[s] for s in range(lru_slabs)]
        final = lax.fori_loop(0, steps, step, tuple(init), unroll=2)
        for s in range(SCAN_SLABS):
            hstate[c0 + s] = final[2 * s]
            hstate[n_slabs + c0 + s] = final[2 * s + 1]
        if with_lru:
            for s in range(lru_slabs):
                lstate[s] = final[2 * SCAN_SLABS + s]

    h_all = jnp.concatenate([h_scr[j].astype(BF16) for j in range(2 * n_slabs)], axis=-1)
    c_all = jnp.concatenate([c_re_ref[...], -c_im_ref[...]], axis=0)
    y = jnp.concatenate([_dot(h_all[r0:r0 + half], c_all) for r0 in (0, half)], axis=0)
    y = _gelu_tanh(y + dskip_ref[...] * us)
    out_scr[0] = y * _sigmoid(_dot(y.astype(BF16), wglu_ref[...]) + bglu_ref[...])
    out_scr[1] = jnp.concatenate([yl_scr[j] for j in range(lru_slabs)], axis=-1)
    for b in range(batch):
        ys_ref[b] = out_scr[0, seq_rows(b), :].astype(ys_ref.dtype)
        yl_ref[b] = out_scr[1, seq_rows(b), :].astype(yl_ref.dtype)


def _recurrent(mix_in, p):
    batch, seq, _ = mix_in.shape
    steps = SCAN_STEPS
    slab_rows = -(-batch * (steps + SEQ_PAD) // LANES) * LANES
    vec = lambda: _const_spec((1, WIDTH))
    sq = lambda: _const_spec((WIDTH, WIDTH))
    out = jax.ShapeDtypeStruct((batch, seq, WIDTH), BF16)
    return pl.pallas_call(
        _scan_body,
        out_shape=(out, out),
        grid=(seq // steps,),
        in_specs=[pl.BlockSpec((batch, steps, WIDTH), lambda t: (0, t, 3)),
                  pl.BlockSpec((batch, steps, WIDTH), lambda t: (0, t, 7)),
                  _const_spec((WIDTH, 2 * SSM_LANES)),
                  _const_spec((batch, SSM_LANES)), _const_spec((batch, SSM_LANES)),
                  _const_spec((SSM_LANES, WIDTH)), _const_spec((SSM_LANES, WIDTH)),
                  vec(), sq(), vec(),
                  _const_spec((CONV_WIDTH, WIDTH)), vec(), sq(), vec(), sq(), vec(), vec()],
        out_specs=(pl.BlockSpec((batch, steps, WIDTH), lambda t: (0, t, 0)),
                   pl.BlockSpec((batch, steps, WIDTH), lambda t: (0, t, 0))),
        scratch_shapes=[pltpu.VMEM((2 * SSM_LANES // LANES, slab_rows, LANES), F32),
                        pltpu.VMEM((2 * SSM_LANES // LANES, slab_rows, LANES), F32),
                        pltpu.VMEM((2 * SSM_LANES // LANES, batch, LANES), F32),
                        pltpu.VMEM((slab_rows, WIDTH), F32),
                        pltpu.VMEM((SUBLANES + slab_rows, WIDTH), F32),
                        pltpu.VMEM((2 * WIDTH // LANES, slab_rows, LANES), F32),
                        pltpu.VMEM((WIDTH // LANES, slab_rows, LANES), F32),
                        pltpu.VMEM((2, slab_rows, WIDTH), F32),
                        pltpu.VMEM((WIDTH // LANES, batch, LANES), F32)],
        compiler_params=_params("arbitrary"),
        name="recurrent",
    )(mix_in, mix_in, p["bbd"], p["lam_re"], p["lam_im"], p["c_re"], p["c_im"], p["d_skip"], p["w_glu"],
      p["b_glu"], p["conv_w"], p["conv_b"], p["w_a"], p["b_a"], p["w_x"], p["b_x"], p["lam"])


def _merge_body(x_ref, pre_g_ref, post_g_ref, y0_ref, y1_ref, y2_ref, y3_ref, wg_ref, wm_ref, wb_ref,
                wo_ref, o_ref):
    x = x_ref[...]
    h = _rms_norm(x, pre_g_ref[...]).astype(BF16)
    merged = None
    for n, y_ref in enumerate((y0_ref, y1_ref, y2_ref, y3_ref)):
        g = _dot(h, wg_ref[:, n * WIDTH:(n + 1) * WIDTH])
        mix = y_ref[...].astype(F32) * (g * _sigmoid(g))
        branch = _dot(mix.astype(BF16), wb_ref[n])
        gate = _sigmoid(_dot(h, wm_ref[:, n * D_MODEL:(n + 1) * D_MODEL]))
        merged = gate * branch if merged is None else merged + gate * branch
    out = _dot(merged.astype(BF16), wo_ref[...])
    o_ref[...] = x + _rms_norm(out, post_g_ref[...])


def _merge(x2, pre_g, post_g, ys, w_gate, w_merge, w_branch, w_out):
    rows = x2.shape[0]
    row_spec = lambda w: pl.BlockSpec((ROW_TILE, w), lambda i: (i, 0))
    return pl.pallas_call(
        _merge_body,
        out_shape=jax.ShapeDtypeStruct((rows, D_MODEL), F32),
        grid=(rows // ROW_TILE,),
        in_specs=[row_spec(D_MODEL), _const_spec((1, D_MODEL)), _const_spec((1, D_MODEL)),
                  row_spec(WIDTH), row_spec(WIDTH), row_spec(WIDTH), row_spec(WIDTH),
                  _const_spec((D_MODEL, GATE_COLS)),
                  _const_spec((D_MODEL, N_BRANCH * D_MODEL)),
                  _const_spec((N_BRANCH, WIDTH, D_MODEL)),
                  _const_spec((D_MODEL, D_MODEL))],
        out_specs=row_spec(D_MODEL),
        compiler_params=_params("parallel"),
        name="merge",
    )(x2, pre_g, post_g, *ys, w_gate, w_merge, w_branch, w_out)


def _block_diag(blocks):
    n, r, c = blocks.shape
    eye = jnp.eye(n, dtype=blocks.dtype)
    return (blocks[:, :, None, :] * eye[:, None, :, None]).reshape(n * r, n * c)


def kernel(x, pre_norm_g, post_norm_g, w_in, ssm_a_re, ssm_a_im, ssm_log_dt, ssm_b_re, ssm_b_im, ssm_c_re,
           ssm_c_im, ssm_d, ssm_w_glu, ssm_b_glu, lru_conv_w, lru_conv_b, lru_w_a, lru_b_a, lru_w_x, lru_b_x,
           lru_lambda, w_branch, w_out):
    batch, seq, _ = x.shape
    depth = w_in.shape[0]
    g, n, c = SSM_GROUPS, SSM_STATE, SSM_GROUP
    assert batch % (2 * SUBLANES) == 0 and batch % ATT_BATCH == 0 and batch % RET_BATCH == 0
    assert seq % max(ATT_TILE, RET_TILE, SCAN_STEPS, ROW_TILE) == 0

    tri = jnp.tril(jnp.ones((ATT_TILE, ATT_TILE), F32), -1).astype(BF16)
    ret_tabs = _retention_tables(seq)

    flat = lambda a: a.reshape(depth * g * n, -1)
    log_dt = jnp.broadcast_to(ssm_log_dt[:, :, None], (depth, g, n))
    lam_re, lam_im, bb_re, bb_im = _s5_prep(flat(ssm_a_re), flat(ssm_a_im), flat(log_dt),
                                             flat(ssm_b_re), flat(ssm_b_im))

    rows = batch * seq
    x2 = x.reshape(rows, D_MODEL)
    for l in range(depth):
        w_l = w_in[l].astype(BF16)
        bb = lambda a: _block_diag(jnp.transpose(a.reshape(depth, g, n, c)[l], (0, 2, 1)))
        lam_rows = lambda a: jnp.broadcast_to(a.reshape(depth, 1, g * n)[l], (batch, g * n))
        rec = {
            "bbd": jnp.concatenate([bb(bb_re), bb(bb_im)], axis=1).astype(BF16),
            "lam_re": lam_rows(lam_re), "lam_im": lam_rows(lam_im),
            "c_re": _block_diag(jnp.transpose(ssm_c_re[l], (0, 2, 1))).astype(BF16),
            "c_im": _block_diag(jnp.transpose(ssm_c_im[l], (0, 2, 1))).astype(BF16),
            "d_skip": ssm_d[l][None, :], "w_glu": ssm_w_glu[l].astype(BF16), "b_glu": ssm_b_glu[l][None, :],
            "conv_w": lru_conv_w[l], "conv_b": lru_conv_b[l][None, :],
            "w_a": _block_diag(lru_w_a[l]).astype(BF16), "b_a": lru_b_a[l].reshape(1, WIDTH),
            "w_x": _block_diag(lru_w_x[l]).astype(BF16), "b_x": lru_b_x[l].reshape(1, WIDTH),
            "lam": lru_lambda[l][None, :],
        }
        pre_g = pre_norm_g[l][None, :]
        mix_in = _inproj(x2, pre_g, w_l[:, :MIX_COLS]).reshape(batch, seq, MIX_COLS)
        y_sb = _stick_breaking(mix_in, tri)
        y_ret = _retention(mix_in, ret_tabs)
        y_ssm, y_lru = _recurrent(mix_in, rec)
        ys = tuple(y.reshape(rows, WIDTH) for y in (y_sb, y_ssm, y_ret, y_lru))
        x2 = _merge(x2, pre_g, post_norm_g[l][None, :], ys,
                    w_l[:, MIX_COLS:MIX_COLS + GATE_COLS], w_l[:, MIX_COLS + GATE_COLS:],
                    w_branch[l].astype(BF16), w_out[l].astype(BF16))
    return x2.reshape(batch, seq, D_MODEL)
```

```python
import functools
import math

import jax
import jax.numpy as jnp
from jax import lax
from jax.experimental import pallas as pl
from jax.experimental.pallas import tpu as pltpu

F32 = jnp.float32
BF16 = jnp.bfloat16

D_MODEL = 1024
N_BRANCH = 4
WIDTH = D_MODEL // N_BRANCH
HEAD_DIM = 64
N_HEADS = WIDTH // HEAD_DIM
SSM_GROUPS = 16
SSM_GROUP = 16
SSM_STATE = 64
SSM_LANES = SSM_GROUPS * SSM_STATE
CONV_WIDTH = 4
LRU_C = 8.0
ROPE_BASE = 10000.0
EPS = 1e-6
LOG2_E = math.log2(math.e)
TINY = 1e-30
MIX_COLS = 8 * WIDTH
GATE_COLS = N_BRANCH * WIDTH

LANES = 128
SUBLANES = 8
ROW_TILE = 1024
ATT_TILE = 256
ATT_BATCH = 4
RET_TILE = 256
RET_BATCH = 8
SCAN_STEPS = 64
SEQ_PAD = 4
SCAN_SLABS = 4
VMEM_LIMIT = 56 * 1024 * 1024


def _dot(a, b):
    return jnp.dot(a, b, preferred_element_type=F32)


def _dot_nt(a, b):
    return lax.dot_general(a, b, (((1,), (1,)), ((), ())), preferred_element_type=F32)


def _sigmoid(x):
    return 0.5 * jnp.tanh(0.5 * x) + 0.5


def _rms_norm(x, g):
    return x * lax.rsqrt(jnp.mean(x * x, axis=-1, keepdims=True) + EPS) * g


def _const_spec(shape):
    nd = len(shape)
    return pl.BlockSpec(shape, lambda *_: (0,) * nd, pipeline_mode=pl.Buffered(1))


def _params(*sem):
    return pltpu.CompilerParams(dimension_semantics=sem, vmem_limit_bytes=VMEM_LIMIT)


def _inproj_body(x_ref, g_ref, w_ref, o_ref):
    h = _rms_norm(x_ref[...], g_ref[...]).astype(BF16)
    o_ref[...] = _dot(h, w_ref[...]).astype(o_ref.dtype)


def _inproj(x2, g, w_mix):
    rows = x2.shape[0]
    return pl.pallas_call(
        _inproj_body,
        out_shape=jax.ShapeDtypeStruct((rows, MIX_COLS), BF16),
        grid=(rows // ROW_TILE,),
        in_specs=[pl.BlockSpec((ROW_TILE, D_MODEL), lambda i: (i, 0)),
                  _const_spec((1, D_MODEL)),
                  _const_spec((D_MODEL, MIX_COLS))],
        out_specs=pl.BlockSpec((ROW_TILE, MIX_COLS), lambda i: (i, 0)),
        compiler_params=_params("parallel"),
        name="inproj",
    )(x2, g, w_mix)


def _sb_body(q_ref, k_ref, v_ref, tri_ref, o_ref, acc_ref):
    i = pl.program_id(1)
    nb, tq, _ = q_ref.shape
    lane_head = lax.broadcasted_iota(jnp.int32, (1, WIDTH), 1) // HEAD_DIM
    row = lax.broadcasted_iota(jnp.int32, (tq, tq), 0)
    col = lax.broadcasted_iota(jnp.int32, (tq, tq), 1)
    causal = col < row
    tri = tri_ref[...]
    acc_ref[...] = jnp.zeros_like(acc_ref)

    zero = jnp.zeros((), BF16)
    head_masks = [lane_head == h for h in range(N_HEADS)]
    z_scale = LOG2_E * HEAD_DIM ** -0.5
    q_heads = [[jnp.where(m, q_ref[n], zero) for m in head_masks] for n in range(nb)]

    def key_tile(j, carries, diagonal):
        k0 = pl.multiple_of(j * tq, tq)
        chains = [(n, h) for n in range(nb) for h in range(N_HEADS)]
        kj = [k_ref[n, pl.ds(k0, tq), :] for n in range(nb)]
        vj = [v_ref[n, pl.ds(k0, tq), :] for n in range(nb)]
        nc = len(chains)
        scores, fails, biases, afters, parts = {}, {}, {}, {}, {}
        for slot in range(nc + 2):
            if slot < nc:
                n, h = chains[slot]
                scores[slot] = _dot_nt(q_heads[n][h], kj[n])
            c = slot - 1
            if 0 <= c < nc:
                z = scores.pop(c) * z_scale
                neg = jnp.minimum(z, 0.0)
                pos = z - neg
                soft = jnp.log2(1.0 + jnp.exp2(neg - pos))
                fail = pos + soft
                if diagonal:
                    fail = jnp.where(causal, fail, 0.0)
                fails[c] = fail.astype(BF16)
                biases[c] = (neg - soft) - carries[c]
                afters[c] = _dot(fails[c], tri)
            c = slot - 2
            if 0 <= c < nc:
                n, h = chains[c]
                w = jnp.exp2(biases.pop(c) - afters[c])
                if diagonal:
                    w = jnp.where(causal, w, 0.0)
                parts[c] = _dot(w.astype(BF16), jnp.where(head_masks[h], vj[n], zero))
        for n in range(nb):
            out = parts[n * N_HEADS]
            for h in range(1, N_HEADS):
                out = out + parts[n * N_HEADS + h]
            acc_ref[n] += out
        return tuple(carry + (afters[c][:, 0:1] + fails[c][:, 0:1].astype(F32)) for c, carry in enumerate(carries))

    carries = key_tile(i, (jnp.zeros((tq, 1), F32),) * (nb * N_HEADS), True)
    lax.fori_loop(0, i, lambda jj, c: key_tile(i - 1 - jj, c, False), carries)
    o_ref[...] = acc_ref[...].astype(o_ref.dtype)


def _stick_breaking(mix_in, tri):
    batch, seq, _ = mix_in.shape
    nb = ATT_BATCH
    return pl.pallas_call(
        _sb_body,
        out_shape=jax.ShapeDtypeStruct((batch, seq, WIDTH), BF16),
        grid=(batch // nb, seq // ATT_TILE),
        in_specs=[pl.BlockSpec((nb, ATT_TILE, WIDTH), lambda b, i: (b, i, 0)),
                  pl.BlockSpec((nb, seq, WIDTH), lambda b, i: (b, 0, 1)),
                  pl.BlockSpec((nb, seq, WIDTH), lambda b, i: (b, 0, 2)),
                  _const_spec((ATT_TILE, ATT_TILE))],
        out_specs=pl.BlockSpec((nb, ATT_TILE, WIDTH), lambda b, i: (b, i, 0)),
        scratch_shapes=[pltpu.VMEM((nb, ATT_TILE, WIDTH), F32)],
        compiler_params=_params("parallel", "arbitrary"),
        name="stick_breaking",
    )(mix_in, mix_in, mix_in, tri)


def _ret_body(q_ref, k_ref, v_ref, cos_ref, sin_ref, rot_ref, decay_ref, xi_ref, zeta_ref, cd_ref,
              bd_ref, avg_ref, o_ref, state_ref):
    @pl.when(pl.program_id(1) == 0)
    def _():
        state_ref[...] = jnp.zeros_like(state_ref)

    cosf = cos_ref[...]
    sinf = sin_ref[...]
    rot = rot_ref[...]
    avg = avg_ref[...]
    lane_head = lax.broadcasted_iota(jnp.int32, (1, WIDTH), 1) // HEAD_DIM
    zero = jnp.zeros((), BF16)
    seqs = range(q_ref.shape[0])
    qs = [q_ref[n] for n in seqs]
    ks = [k_ref[n] for n in seqs]
    vs = [v_ref[n] for n in seqs]
    q_rot = [_dot(qs[n], rot) for n in seqs]
    k_rot = [_dot(ks[n], rot) for n in seqs]
    qr = [qs[n].astype(F32) * cosf + q_rot[n] * sinf for n in seqs]
    kr = [(ks[n].astype(F32) * cosf + k_rot[n] * sinf) * (HEAD_DIM ** -0.5) for n in seqs]
    qrb = [x.astype(BF16) for x in qr]
    krb = [x.astype(BF16) for x in kr]
    head_masks = [lane_head == h for h in range(N_HEADS)]
    scores = [[_dot_nt(jnp.where(m, qrb[n], zero), krb[n]) for m in head_masks] for n in seqs]
    states = [state_ref[n] for n in seqs]
    cross = [_dot((qr[n] * xi_ref[...]).astype(BF16), states[n].astype(BF16)) for n in seqs]
    kv = [_dot((kr[n] * zeta_ref[...]).T.astype(BF16), vs[n]) for n in seqs]
    outs = []
    for n in seqs:
        o = cross[n]
        for h, m in enumerate(head_masks):
            o = o + _dot((scores[n][h] * decay_ref[h]).astype(BF16), jnp.where(m, vs[n], zero))
        outs.append(o)
        state_ref[n] = cd_ref[...] * states[n] + kv[n] * bd_ref[...]
    means = [_dot(o.astype(BF16), avg) for o in outs]
    devs = [outs[n] - means[n] for n in seqs]
    variances = [_dot((d * d).astype(BF16), avg) for d in devs]
    for n in seqs:
        o_ref[n] = (devs[n] * lax.rsqrt(variances[n] + EPS)).astype(o_ref.dtype)


def _retention(mix_in, tabs):
    batch, seq, _ = mix_in.shape
    c = RET_TILE
    nb = RET_BATCH
    tile = lambda col: pl.BlockSpec((nb, c, WIDTH), lambda b, n: (b, n, col))
    return pl.pallas_call(
        _ret_body,
        out_shape=jax.ShapeDtypeStruct((batch, seq, WIDTH), BF16),
        grid=(batch // nb, seq // c),
        in_specs=[tile(4), tile(5), tile(6),
                  pl.BlockSpec((c, WIDTH), lambda b, n: (n, 0)),
                  pl.BlockSpec((c, WIDTH), lambda b, n: (n, 0)),
                  _const_spec((WIDTH, WIDTH)),
                  _const_spec((N_HEADS, c, c)),
                  _const_spec((c, WIDTH)),
                  _const_spec((c, WIDTH)),
                  _const_spec((WIDTH, 1)),
                  _const_spec((WIDTH, WIDTH)),
                  _const_spec((WIDTH, WIDTH))],
        out_specs=pl.BlockSpec((nb, c, WIDTH), lambda b, n: (b, n, 0)),
        scratch_shapes=[pltpu.VMEM((nb, WIDTH, WIDTH), F32)],
        compiler_params=_params("parallel", "arbitrary"),
        name="retention",
    )(mix_in, mix_in, mix_in, tabs["cos"], tabs["sin"], tabs["rot"], tabs["decay"], tabs["xi"],
      tabs["zeta"], tabs["cd"], tabs["bd"], tabs["avg"])


def _retention_tables(seq):
    c = RET_TILE
    half = HEAD_DIM // 2
    inv_freq = ROPE_BASE ** (-jnp.arange(half, dtype=F32) / half)
    ang = jnp.arange(seq, dtype=F32)[:, None] * inv_freq[None, :]
    cos = jnp.tile(jnp.cos(ang), (1, 2 * N_HEADS))
    sin = jnp.tile(jnp.sin(ang), (1, 2 * N_HEADS))
    lane = jnp.arange(WIDTH)
    d = lane % HEAD_DIM
    src = jnp.where(d < half, lane + half, lane - half)
    sign = jnp.where(d < half, -1.0, 1.0)
    rot = (jnp.zeros((WIDTH, WIDTH), F32).at[src, lane].set(sign)).astype(BF16)
    log_g = jnp.log1p(-(2.0 ** (-5.0 - jnp.arange(N_HEADS, dtype=F32))))
    i = jnp.arange(c, dtype=F32)
    rel = i[:, None] - i[None, :]
    decay = jnp.where(rel >= 0, jnp.exp(log_g[:, None, None] * jnp.maximum(rel, 0.0)), 0.0)
    log_g_lane = log_g[lane // HEAD_DIM]
    xi = jnp.exp(log_g_lane[None, :] * (i + 1.0)[:, None])
    zeta = jnp.exp(log_g_lane[None, :] * (c - 1 - i)[:, None])
    cd = jnp.exp(log_g_lane * c)[:, None]
    same_head = (lane[:, None] // HEAD_DIM) == (lane[None, :] // HEAD_DIM)
    return {"cos": cos, "sin": sin, "rot": rot, "decay": decay, "xi": xi, "zeta": zeta, "cd": cd,
            "bd": same_head.astype(F32), "avg": (same_head.astype(F32) / HEAD_DIM).astype(BF16)}


def _s5_prep_body(a_re_ref, a_im_ref, log_dt_ref, b_re_ref, b_im_ref, lam_re_ref, lam_im_ref,
                  bb_re_ref, bb_im_ref):
    a_re = a_re_ref[...]
    a_im = a_im_ref[...]
    dt = jnp.exp(log_dt_ref[...])
    mag = jnp.exp(dt * a_re)
    ab_re = mag * jnp.cos(dt * a_im)
    ab_im = mag * jnp.sin(dt * a_im)
    den = a_re * a_re + a_im * a_im
    num_re = ab_re - 1.0
    f_re = (num_re * a_re + ab_im * a_im) / den
    f_im = (ab_im * a_re - num_re * a_im) / den
    b_re = b_re_ref[...]
    b_im = b_im_ref[...]
    lam_re_ref[...] = ab_re
    lam_im_ref[...] = ab_im
    bb_re_ref[...] = f_re * b_re - f_im * b_im
    bb_im_ref[...] = f_re * b_im + f_im * b_re


def _s5_prep(a_re, a_im, log_dt, b_re, b_im):
    rows = a_re.shape[0]
    col = jax.ShapeDtypeStruct((rows, 1), F32)
    mat = jax.ShapeDtypeStruct((rows, SSM_GROUP), F32)
    return pl.pallas_call(_s5_prep_body, out_shape=(col, col, mat, mat), name="s5_prep")(
        a_re, a_im, log_dt, b_re, b_im)


def _gelu_tanh(y):
    return 0.5 * y * (1.0 + jnp.tanh(math.sqrt(2.0 / math.pi) * (y + 0.044715 * (y * y * y))))


def _scan_body(us_ref, ul_ref, bbd_ref, lam_re_ref, lam_im_ref, c_re_ref, c_im_ref, dskip_ref, wglu_ref,
               bglu_ref, convw_ref, convb_ref, wa_ref, ba_ref, wx_ref, bx_ref, lam_lru_ref,
               ys_ref, yl_ref,
               x_scr, h_scr, hstate, us_scr, ul_scr, ab_scr, yl_scr, out_scr, lstate):
    batch, steps, _ = us_ref.shape
    n_slabs = SSM_LANES // LANES
    lru_slabs = WIDTH // LANES
    pitch = steps + SEQ_PAD
    slab_rows = x_scr.shape[1]
    lead = SUBLANES
    step_rows = lambda t: pl.ds(SEQ_PAD + t, batch, stride=pitch)
    seq_rows = lambda b: slice(b * pitch + SEQ_PAD, (b + 1) * pitch)

    @pl.when(pl.program_id(0) == 0)
    def _():
        hstate[...] = jnp.zeros_like(hstate)
        lstate[...] = jnp.zeros_like(lstate)
        us_scr[...] = jnp.zeros_like(us_scr)
        ul_scr[...] = jnp.zeros_like(ul_scr)
        h_scr[...] = jnp.zeros_like(h_scr)
        yl_scr[...] = jnp.zeros_like(yl_scr)

    for b in range(batch):
        us_scr[seq_rows(b), :] = us_ref[b].astype(F32)
        ul_scr[lead + b * pitch + SEQ_PAD:lead + (b + 1) * pitch, :] = ul_ref[b].astype(F32)
    us = us_scr[...]
    usb = us.astype(BF16)
    half = slab_rows // 2
    chunk = slab_rows // n_slabs
    neg_lam = -lam_lru_ref[...]
    softplus = jnp.maximum(neg_lam, 0.0) + jnp.log(1.0 + jnp.exp(-jnp.abs(neg_lam)))
    for jj in range(n_slabs):
        for r0 in (0, half):
            part = _dot(usb[r0:r0 + half], bbd_ref[:, 2 * jj * LANES:(2 * jj + 2) * LANES])
            for j in range(2):
                x_scr[2 * jj + j, r0:r0 + half, :] = part[:, j * LANES:(j + 1) * LANES]
        r0 = jj * chunk
        xc = convb_ref[...]
        for kk in range(CONV_WIDTH):
            lo = lead - (CONV_WIDTH - 1) + kk + r0
            xc = xc + convw_ref[kk:kk + 1, :] * ul_scr[lo:lo + chunk, :]
        xcb = xc.astype(BF16)
        r = _sigmoid(_dot(xcb, wa_ref[...]) + ba_ref[...])
        ig = _sigmoid(_dot(xcb, wx_ref[...]) + bx_ref[...])
        a = jnp.exp(-LRU_C * r * softplus)
        gap = 1.0 - a * a
        bterm = gap * lax.rsqrt(jnp.maximum(gap, TINY)) * (ig * xc)
        for j in range(lru_slabs):
            ab_scr[j, r0:r0 + chunk, :] = a[:, j * LANES:(j + 1) * LANES]
            ab_scr[lru_slabs + j, r0:r0 + chunk, :] = bterm[:, j * LANES:(j + 1) * LANES]
    for b in range(batch):
        ul_scr[lead + b * pitch:lead + b * pitch + SEQ_PAD, :] = (
            ul_scr[lead + (b + 1) * pitch - SEQ_PAD:lead + (b + 1) * pitch, :])

    for c0 in range(0, n_slabs, SCAN_SLABS):
        lam = [(lam_re_ref[:, (c0 + s) * LANES:(c0 + s + 1) * LANES],
                lam_im_ref[:, (c0 + s) * LANES:(c0 + s + 1) * LANES]) for s in range(SCAN_SLABS)]
        with_lru = c0 == 0

        def step(t, carry, c0=c0, lam=lam, with_lru=with_lru):
            new = []
            for s in range(SCAN_SLABS):
                h_re, h_im = carry[2 * s], carry[2 * s + 1]
                lam_re, lam_im = lam[s]
                n_re = lam_re * h_re - lam_im * h_im + x_scr.at[c0 + s][step_rows(t), :]
                n_im = lam_re * h_im + lam_im * h_re + x_scr.at[n_slabs + c0 + s][step_rows(t), :]
                h_scr.at[c0 + s][step_rows(t), :] = n_re
                h_scr.at[n_slabs + c0 + s][step_rows(t), :] = n_im
                new += [n_re, n_im]
            if with_lru:
                for s in range(lru_slabs):
                    h = (ab_scr.at[s][step_rows(t), :] * carry[2 * SCAN_SLABS + s]
                         + ab_scr.at[lru_slabs + s][step_rows(t), :])
                    yl_scr.at[s][step_rows(t), :] = h
                    new.append(h)
            return tuple(new)

        init = []
        for s in range(SCAN_SLABS):
            init += [hstate[c0 + s], hstate[n_slabs + c0 + s]]
        if with_lru:
            init += [lstate[s] for s in range(lru_slabs)]
        final = lax.fori_loop(0, steps, step, tuple(init), unroll=2)
        for s in range(SCAN_SLABS):
            hstate[c0 + s] = final[2 * s]
            hstate[n_slabs + c0 + s] = final[2 * s + 1]
        if with_lru:
            for s in range(lru_slabs):
                lstate[s] = final[2 * SCAN_SLABS + s]

    h_all = jnp.concatenate([h_scr[j].astype(BF16) for j in range(2 * n_slabs)], axis=-1)
    c_all = jnp.concatenate([c_re_ref[...], -c_im_ref[...]], axis=0)
    y = jnp.concatenate([_dot(h_all[r0:r0 + half], c_all) for r0 in (0, half)], axis=0)
    y = _gelu_tanh(y + dskip_ref[...] * us)
    out_scr[0] = y * _sigmoid(_dot(y.astype(BF16), wglu_ref[...]) + bglu_ref[...])
    out_scr[1] = jnp.concatenate([yl_scr[j] for j in range(lru_slabs)], axis=-1)
    for b in range(batch):
        ys_ref[b] = out_scr[0, seq_rows(b), :].astype(ys_ref.dtype)
        yl_ref[b] = out_scr[1, seq_rows(b), :].astype(yl_ref.dtype)


def _recurrent(mix_in, p):
    batch, seq, _ = mix_in.shape
    steps = SCAN_STEPS
    slab_rows = -(-batch * (steps + SEQ_PAD) // LANES) * LANES
    vec = lambda: _const_spec((1, WIDTH))
    sq = lambda: _const_spec((WIDTH, WIDTH))
    out = jax.ShapeDtypeStruct((batch, seq, WIDTH), BF16)
    return pl.pallas_call(
        _scan_body,
        out_shape=(out, out),
        grid=(seq // steps,),
        in_specs=[pl.BlockSpec((batch, steps, WIDTH), lambda t: (0, t, 3)),
                  pl.BlockSpec((batch, steps, WIDTH), lambda t: (0, t, 7)),
                  _const_spec((WIDTH, 2 * SSM_LANES)),
                  _const_spec((batch, SSM_LANES)), _const_spec((batch, SSM_LANES)),
                  _const_spec((SSM_LANES, WIDTH)), _const_spec((SSM_LANES, WIDTH)),
                  vec(), sq(), vec(),
                  _const_spec((CONV_WIDTH, WIDTH)), vec(), sq(), vec(), sq(), vec(), vec()],
        out_specs=(pl.BlockSpec((batch, steps, WIDTH), lambda t: (0, t, 0)),
                   pl.BlockSpec((batch, steps, WIDTH), lambda t: (0, t, 0))),
        scratch_shapes=[pltpu.VMEM((2 * SSM_LANES // LANES, slab_rows, LANES), F32),
                        pltpu.VMEM((2 * SSM_LANES // LANES, slab_rows, LANES), F32),
                        pltpu.VMEM((2 * SSM_LANES // LANES, batch, LANES), F32),
                        pltpu.VMEM((slab_rows, WIDTH), F32),
                        pltpu.VMEM((SUBLANES + slab_rows, WIDTH), F32),
                        pltpu.VMEM((2 * WIDTH // LANES, slab_rows, LANES), F32),
                        pltpu.VMEM((WIDTH // LANES, slab_rows, LANES), F32),
                        pltpu.VMEM((2, slab_rows, WIDTH), F32),
                        pltpu.VMEM((WIDTH // LANES, batch, LANES), F32)],
        compiler_params=_params("arbitrary"),
        name="recurrent",
    )(mix_in, mix_in, p["bbd"], p["lam_re"], p["lam_im"], p["c_re"], p["c_im"], p["d_skip"], p["w_glu"],
      p["b_glu"], p["conv_w"], p["conv_b"], p["w_a"], p["b_a"], p["w_x"], p["b_x"], p["lam"])


def _merge_body(x_ref, pre_g_ref, post_g_ref, y0_ref, y1_ref, y2_ref, y3_ref, wg_ref, wm_ref, wb_ref,
                wo_ref, o_ref):
    x = x_ref[...]
    h = _rms_norm(x, pre_g_ref[...]).astype(BF16)
    merged = None
    for n, y_ref in enumerate((y0_ref, y1_ref, y2_ref, y3_ref)):
        g = _dot(h, wg_ref[:, n * WIDTH:(n + 1) * WIDTH])
        mix = y_ref[...].astype(F32) * (g * _sigmoid(g))
        branch = _dot(mix.astype(BF16), wb_ref[n])
        gate = _sigmoid(_dot(h, wm_ref[:, n * D_MODEL:(n + 1) * D_MODEL]))
        merged = gate * branch if merged is None else merged + gate * branch
    out = _dot(merged.astype(BF16), wo_ref[...])
    o_ref[...] = x + _rms_norm(out, post_g_ref[...])


def _merge(x2, pre_g, post_g, ys, w_gate, w_merge, w_branch, w_out):
    rows = x2.shape[0]
    row_spec = lambda w: pl.BlockSpec((ROW_TILE, w), lambda i: (i, 0))
    return pl.pallas_call(
        _merge_body,
        out_shape=jax.ShapeDtypeStruct((rows, D_MODEL), F32),
        grid=(rows // ROW_TILE,),
        in_specs=[row_spec(D_MODEL), _const_spec((1, D_MODEL)), _const_spec((1, D_MODEL)),
                  row_spec(WIDTH), row_spec(WIDTH), row_spec(WIDTH), row_spec(WIDTH),
                  _const_spec((D_MODEL, GATE_COLS)),
                  _const_spec((D_MODEL, N_BRANCH * D_MODEL)),
                  _const_spec((N_BRANCH, WIDTH, D_MODEL)),
                  _const_spec((D_MODEL, D_MODEL))],
        out_specs=row_spec(D_MODEL),
        compiler_params=_params("parallel"),
        name="merge",
    )(x2, pre_g, post_g, *ys, w_gate, w_merge, w_branch, w_out)


def _block_diag(blocks):
    n, r, c = blocks.shape
    eye = jnp.eye(n, dtype=blocks.dtype)
    return (blocks[:, :, None, :] * eye[:, None, :, None]).reshape(n * r, n * c)


def kernel(x, pre_norm_g, post_norm_g, w_in, ssm_a_re, ssm_a_im, ssm_log_dt, ssm_b_re, ssm_b_im, ssm_c_re,
           ssm_c_im, ssm_d, ssm_w_glu, ssm_b_glu, lru_conv_w, lru_conv_b, lru_w_a, lru_b_a, lru_w_x, lru_b_x,
           lru_lambda, w_branch, w_out):
    batch, seq, _ = x.shape
    depth = w_in.shape[0]
    g, n, c = SSM_GROUPS, SSM_STATE, SSM_GROUP
    assert batch % (2 * SUBLANES) == 0 and batch % ATT_BATCH == 0 and batch % RET_BATCH == 0
    assert seq % max(ATT_TILE, RET_TILE, SCAN_STEPS) == 0 and (batch * seq) % ROW_TILE == 0

    tri = jnp.tril(jnp.ones((ATT_TILE, ATT_TILE), F32), -1).astype(BF16)
    ret_tabs = _retention_tables(seq)

    flat = lambda a: a.reshape(depth * g * n, -1)
    log_dt = jnp.broadcast_to(ssm_log_dt[:, :, None], (depth, g, n))
    lam_re, lam_im, bb_re, bb_im = _s5_prep(flat(ssm_a_re), flat(ssm_a_im), flat(log_dt),
                                             flat(ssm_b_re), flat(ssm_b_im))

    rows = batch * seq
    x2 = x.reshape(rows, D_MODEL)
    for l in range(depth):
        w_l = w_in[l].astype(BF16)
        bb = lambda a: _block_diag(jnp.transpose(a.reshape(depth, g, n, c)[l], (0, 2, 1)))
        lam_rows = lambda a: jnp.broadcast_to(a.reshape(depth, 1, g * n)[l], (batch, g * n))
        rec = {
            "bbd": jnp.concatenate([bb(bb_re), bb(bb_im)], axis=1).astype(BF16),
            "lam_re": lam_rows(lam_re), "lam_im": lam_rows(lam_im),
            "c_re": _block_diag(jnp.transpose(ssm_c_re[l], (0, 2, 1))).astype(BF16),
            "c_im": _block_diag(jnp.transpose(ssm_c_im[l], (0, 2, 1))).astype(BF16),
            "d_skip": ssm_d[l][None, :], "w_glu": ssm_w_glu[l].astype(BF16), "b_glu": ssm_b_glu[l][None, :],
            "conv_w": lru_conv_w[l], "conv_b": lru_conv_b[l][None, :],
            "w_a": _block_diag(lru_w_a[l]).astype(BF16), "b_a": lru_b_a[l].reshape(1, WIDTH),
            "w_x": _block_diag(lru_w_x[l]).astype(BF16), "b_x": lru_b_x[l].reshape(1, WIDTH),
            "lam": lru_lambda[l][None, :],
        }
        pre_g = pre_norm_g[l][None, :]
        mix_in = _inproj(x2, pre_g, w_l[:, :MIX_COLS]).reshape(batch, seq, MIX_COLS)
        y_sb = _stick_breaking(mix_in, tri)
        y_ret = _retention(mix_in, ret_tabs)
        y_ssm, y_lru = _recurrent(mix_in, rec)
        ys = tuple(y.reshape(rows, WIDTH) for y in (y_sb, y_ssm, y_ret, y_lru))
        x2 = _merge(x2, pre_g, post_norm_g[l][None, :], ys,
                    w_l[:, MIX_COLS:MIX_COLS + GATE_COLS], w_l[:, MIX_COLS + GATE_COLS:],
                    w_branch[l].astype(BF16), w_out[l].astype(BF16))
    return x2.reshape(batch, seq, D_MODEL)
```

```python
import functools
import math

import jax
import jax.numpy as jnp
from jax import lax
from jax.experimental import pallas as pl
from jax.experimental.pallas import tpu as pltpu

F32 = jnp.float32
BF16 = jnp.bfloat16

D_MODEL = 1024
N_BRANCH = 4
WIDTH = D_MODEL // N_BRANCH
HEAD_DIM = 64
N_HEADS = WIDTH // HEAD_DIM
SSM_GROUPS = 16
SSM_GROUP = 16
SSM_STATE = 64
SSM_LANES = SSM_GROUPS * SSM_STATE
CONV_WIDTH = 4
LRU_C = 8.0
ROPE_BASE = 10000.0
EPS = 1e-6
LOG2_E = math.log2(math.e)
TINY = 1e-30
MIX_COLS = 8 * WIDTH
GATE_COLS = N_BRANCH * WIDTH

LANES = 128
SUBLANES = 8
ROW_TILE = 1024
ATT_TILE = 256
ATT_BATCH = 8
RET_TILE = 256
RET_BATCH = 8
SCAN_STEPS = 64
SEQ_PAD = 4
SCAN_SLABS = 4
VMEM_LIMIT = 56 * 1024 * 1024


def _dot(a, b):
    return jnp.dot(a, b, preferred_element_type=F32)


def _dot_nt(a, b):
    return lax.dot_general(a, b, (((1,), (1,)), ((), ())), preferred_element_type=F32)


def _sigmoid(x):
    return 0.5 * jnp.tanh(0.5 * x) + 0.5


def _rms_norm(x, g):
    return x * lax.rsqrt(jnp.mean(x * x, axis=-1, keepdims=True) + EPS) * g


def _const_spec(shape):
    nd = len(shape)
    return pl.BlockSpec(shape, lambda *_: (0,) * nd, pipeline_mode=pl.Buffered(1))


def _params(*sem):
    return pltpu.CompilerParams(dimension_semantics=sem, vmem_limit_bytes=VMEM_LIMIT)


def _inproj_body(x_ref, g_ref, w_ref, o_ref):
    h = _rms_norm(x_ref[...], g_ref[...]).astype(BF16)
    o_ref[...] = _dot(h, w_ref[...]).astype(o_ref.dtype)


def _inproj(x2, g, w_mix):
    rows = x2.shape[0]
    return pl.pallas_call(
        _inproj_body,
        out_shape=jax.ShapeDtypeStruct((rows, MIX_COLS), BF16),
        grid=(rows // ROW_TILE,),
        in_specs=[pl.BlockSpec((ROW_TILE, D_MODEL), lambda i: (i, 0)),
                  _const_spec((1, D_MODEL)),
                  _const_spec((D_MODEL, MIX_COLS))],
        out_specs=pl.BlockSpec((ROW_TILE, MIX_COLS), lambda i: (i, 0)),
        compiler_params=_params("parallel"),
        name="inproj",
    )(x2, g, w_mix)


def _sb_body(q_ref, k_ref, v_ref, tri_ref, o_ref, acc_ref):
    i = pl.program_id(1)
    nb, tq, _ = q_ref.shape
    lane_head = lax.broadcasted_iota(jnp.int32, (1, WIDTH), 1) // HEAD_DIM
    row = lax.broadcasted_iota(jnp.int32, (tq, tq), 0)
    col = lax.broadcasted_iota(jnp.int32, (tq, tq), 1)
    causal = col < row
    tri = tri_ref[...]
    acc_ref[...] = jnp.zeros_like(acc_ref)

    zero = jnp.zeros((), BF16)
    head_masks = [lane_head == h for h in range(N_HEADS)]
    z_scale = LOG2_E * HEAD_DIM ** -0.5
    q_heads = [[jnp.where(m, q_ref[n], zero) for m in head_masks] for n in range(nb)]

    def key_tile(j, carries, diagonal):
        k0 = pl.multiple_of(j * tq, tq)
        chains = [(n, h) for n in range(nb) for h in range(N_HEADS)]
        kj = [k_ref[n, pl.ds(k0, tq), :] for n in range(nb)]
        vj = [v_ref[n, pl.ds(k0, tq), :] for n in range(nb)]
        nc = len(chains)
        scores, fails, biases, afters, parts = {}, {}, {}, {}, {}
        for slot in range(nc + 2):
            if slot < nc:
                n, h = chains[slot]
                scores[slot] = _dot_nt(q_heads[n][h], kj[n])
            c = slot - 1
            if 0 <= c < nc:
                z = scores.pop(c) * z_scale
                neg = jnp.minimum(z, 0.0)
                pos = z - neg
                soft = jnp.log2(1.0 + jnp.exp2(neg - pos))
                fail = pos + soft
                if diagonal:
                    fail = jnp.where(causal, fail, 0.0)
                fails[c] = fail.astype(BF16)
                biases[c] = (neg - soft) - carries[c]
                afters[c] = _dot(fails[c], tri)
            c = slot - 2
            if 0 <= c < nc:
                n, h = chains[c]
                w = jnp.exp2(biases.pop(c) - afters[c])
                if diagonal:
                    w = jnp.where(causal, w, 0.0)
                parts[c] = _dot(w.astype(BF16), jnp.where(head_masks[h], vj[n], zero))
        for n in range(nb):
            out = parts[n * N_HEADS]
            for h in range(1, N_HEADS):
                out = out + parts[n * N_HEADS + h]
            acc_ref[n] += out
        return tuple(carry + (afters[c][:, 0:1] + fails[c][:, 0:1].astype(F32)) for c, carry in enumerate(carries))

    carries = key_tile(i, (jnp.zeros((tq, 1), F32),) * (nb * N_HEADS), True)
    lax.fori_loop(0, i, lambda jj, c: key_tile(i - 1 - jj, c, False), carries)
    o_ref[...] = acc_ref[...].astype(o_ref.dtype)


def _stick_breaking(mix_in, tri):
    batch, seq, _ = mix_in.shape
    nb = ATT_BATCH
    return pl.pallas_call(
        _sb_body,
        out_shape=jax.ShapeDtypeStruct((batch, seq, WIDTH), BF16),
        grid=(batch // nb, seq // ATT_TILE),
        in_specs=[pl.BlockSpec((nb, ATT_TILE, WIDTH), lambda b, i: (b, i, 0)),
                  pl.BlockSpec((nb, seq, WIDTH), lambda b, i: (b, 0, 1)),
                  pl.BlockSpec((nb, seq, WIDTH), lambda b, i: (b, 0, 2)),
                  _const_spec((ATT_TILE, ATT_TILE))],
        out_specs=pl.BlockSpec((nb, ATT_TILE, WIDTH), lambda b, i: (b, i, 0)),
        scratch_shapes=[pltpu.VMEM((nb, ATT_TILE, WIDTH), F32)],
        compiler_params=_params("parallel", "arbitrary"),
        name="stick_breaking",
    )(mix_in, mix_in, mix_in, tri)


def _ret_body(q_ref, k_ref, v_ref, cos_ref, sin_ref, rot_ref, decay_ref, xi_ref, zeta_ref, cd_ref,
              bd_ref, avg_ref, o_ref, state_ref):
    @pl.when(pl.program_id(1) == 0)
    def _():
        state_ref[...] = jnp.zeros_like(state_ref)

    cosf = cos_ref[...]
    sinf = sin_ref[...]
    rot = rot_ref[...]
    avg = avg_ref[...]
    lane_head = lax.broadcasted_iota(jnp.int32, (1, WIDTH), 1) // HEAD_DIM
    zero = jnp.zeros((), BF16)
    seqs = range(q_ref.shape[0])
    qs = [q_ref[n] for n in seqs]
    ks = [k_ref[n] for n in seqs]
    vs = [v_ref[n] for n in seqs]
    q_rot = [_dot(qs[n], rot) for n in seqs]
    k_rot = [_dot(ks[n], rot) for n in seqs]
    qr = [qs[n].astype(F32) * cosf + q_rot[n] * sinf for n in seqs]
    kr = [(ks[n].astype(F32) * cosf + k_rot[n] * sinf) * (HEAD_DIM ** -0.5) for n in seqs]
    qrb = [x.astype(BF16) for x in qr]
    krb = [x.astype(BF16) for x in kr]
    head_masks = [lane_head == h for h in range(N_HEADS)]
    scores = [[_dot_nt(jnp.where(m, qrb[n], zero), krb[n]) for m in head_masks] for n in seqs]
    states = [state_ref[n] for n in seqs]
    cross = [_dot((qr[n] * xi_ref[...]).astype(BF16), states[n].astype(BF16)) for n in seqs]
    kv = [_dot((kr[n] * zeta_ref[...]).T.astype(BF16), vs[n]) for n in seqs]
    outs = []
    for n in seqs:
        o = cross[n]
        for h, m in enumerate(head_masks):
            o = o + _dot((scores[n][h] * decay_ref[h]).astype(BF16), jnp.where(m, vs[n], zero))
        outs.append(o)
        state_ref[n] = cd_ref[...] * states[n] + kv[n] * bd_ref[...]
    means = [_dot(o.astype(BF16), avg) for o in outs]
    devs = [outs[n] - means[n] for n in seqs]
    variances = [_dot((d * d).astype(BF16), avg) for d in devs]
    for n in seqs:
        o_ref[n] = (devs[n] * lax.rsqrt(variances[n] + EPS)).astype(o_ref.dtype)


def _retention(mix_in, tabs):
    batch, seq, _ = mix_in.shape
    c = RET_TILE
    nb = RET_BATCH
    tile = lambda col: pl.BlockSpec((nb, c, WIDTH), lambda b, n: (b, n, col))
    return pl.pallas_call(
        _ret_body,
        out_shape=jax.ShapeDtypeStruct((batch, seq, WIDTH), BF16),
        grid=(batch // nb, seq // c),
        in_specs=[tile(4), tile(5), tile(6),
                  pl.BlockSpec((c, WIDTH), lambda b, n: (n, 0)),
                  pl.BlockSpec((c, WIDTH), lambda b, n: (n, 0)),
                  _const_spec((WIDTH, WIDTH)),
                  _const_spec((N_HEADS, c, c)),
                  _const_spec((c, WIDTH)),
                  _const_spec((c, WIDTH)),
                  _const_spec((WIDTH, 1)),
                  _const_spec((WIDTH, WIDTH)),
                  _const_spec((WIDTH, WIDTH))],
        out_specs=pl.BlockSpec((nb, c, WIDTH), lambda b, n: (b, n, 0)),
        scratch_shapes=[pltpu.VMEM((nb, WIDTH, WIDTH), F32)],
        compiler_params=_params("parallel", "arbitrary"),
        name="retention",
    )(mix_in, mix_in, mix_in, tabs["cos"], tabs["sin"], tabs["rot"], tabs["decay"], tabs["xi"],
      tabs["zeta"], tabs["cd"], tabs["bd"], tabs["avg"])


def _retention_tables(seq):
    c = RET_TILE
    half = HEAD_DIM // 2
    inv_freq = ROPE_BASE ** (-jnp.arange(half, dtype=F32) / half)
    ang = jnp.arange(seq, dtype=F32)[:, None] * inv_freq[None, :]
    cos = jnp.tile(jnp.cos(ang), (1, 2 * N_HEADS))
    sin = jnp.tile(jnp.sin(ang), (1, 2 * N_HEADS))
    lane = jnp.arange(WIDTH)
    d = lane % HEAD_DIM
    src = jnp.where(d < half, lane + half, lane - half)
    sign = jnp.where(d < half, -1.0, 1.0)
    rot = (jnp.zeros((WIDTH, WIDTH), F32).at[src, lane].set(sign)).astype(BF16)
    log_g = jnp.log1p(-(2.0 ** (-5.0 - jnp.arange(N_HEADS, dtype=F32))))
    i = jnp.arange(c, dtype=F32)
    rel = i[:, None] - i[None, :]
    decay = jnp.where(rel >= 0, jnp.exp(log_g[:, None, None] * jnp.maximum(rel, 0.0)), 0.0)
    log_g_lane = log_g[lane // HEAD_DIM]
    xi = jnp.exp(log_g_lane[None, :] * (i + 1.0)[:, None])
    zeta = jnp.exp(log_g_lane[None, :] * (c - 1 - i)[:, None])
    cd = jnp.exp(log_g_lane * c)[:, None]
    same_head = (lane[:, None] // HEAD_DIM) == (lane[None, :] // HEAD_DIM)
    return {"cos": cos, "sin": sin, "rot": rot, "decay": decay, "xi": xi, "zeta": zeta, "cd": cd,
            "bd": same_head.astype(F32), "avg": (same_head.astype(F32) / HEAD_DIM).astype(BF16)}


def _s5_prep_body(a_re_ref, a_im_ref, log_dt_ref, b_re_ref, b_im_ref, lam_re_ref, lam_im_ref,
                  bb_re_ref, bb_im_ref):
    a_re = a_re_ref[...]
    a_im = a_im_ref[...]
    dt = jnp.exp(log_dt_ref[...])
    mag = jnp.exp(dt * a_re)
    ab_re = mag * jnp.cos(dt * a_im)
    ab_im = mag * jnp.sin(dt * a_im)
    den = a_re * a_re + a_im * a_im
    num_re = ab_re - 1.0
    f_re = (num_re * a_re + ab_im * a_im) / den
    f_im = (ab_im * a_re - num_re * a_im) / den
    b_re = b_re_ref[...]
    b_im = b_im_ref[...]
    lam_re_ref[...] = ab_re
    lam_im_ref[...] = ab_im
    bb_re_ref[...] = f_re * b_re - f_im * b_im
    bb_im_ref[...] = f_re * b_im + f_im * b_re


def _s5_prep(a_re, a_im, log_dt, b_re, b_im):
    rows = a_re.shape[0]
    col = jax.ShapeDtypeStruct((rows, 1), F32)
    mat = jax.ShapeDtypeStruct((rows, SSM_GROUP), F32)
    return pl.pallas_call(_s5_prep_body, out_shape=(col, col, mat, mat), name="s5_prep")(
        a_re, a_im, log_dt, b_re, b_im)


def _gelu_tanh(y):
    return 0.5 * y * (1.0 + jnp.tanh(math.sqrt(2.0 / math.pi) * (y + 0.044715 * (y * y * y))))


def _scan_body(us_ref, ul_ref, bbd_ref, lam_re_ref, lam_im_ref, c_re_ref, c_im_ref, dskip_ref, wglu_ref,
               bglu_ref, convw_ref, convb_ref, wa_ref, ba_ref, wx_ref, bx_ref, lam_lru_ref,
               ys_ref, yl_ref,
               x_scr, h_scr, hstate, us_scr, ul_scr, ab_scr, yl_scr, out_scr, lstate):
    batch, steps, _ = us_ref.shape
    n_slabs = SSM_LANES // LANES
    lru_slabs = WIDTH // LANES
    pitch = steps + SEQ_PAD
    slab_rows = x_scr.shape[1]
    lead = SUBLANES
    step_rows = lambda t: pl.ds(SEQ_PAD + t, batch, stride=pitch)
    seq_rows = lambda b: slice(b * pitch + SEQ_PAD, (b + 1) * pitch)

    @pl.when(pl.program_id(0) == 0)
    def _():
        hstate[...] = jnp.zeros_like(hstate)
        lstate[...] = jnp.zeros_like(lstate)
        us_scr[...] = jnp.zeros_like(us_scr)
        ul_scr[...] = jnp.zeros_like(ul_scr)
        h_scr[...] = jnp.zeros_like(h_scr)
        yl_scr[...] = jnp.zeros_like(yl_scr)

    for b in range(batch):
        us_scr[seq_rows(b), :] = us_ref[b].astype(F32)
        ul_scr[lead + b * pitch + SEQ_PAD:lead + (b + 1) * pitch, :] = ul_ref[b].astype(F32)
    us = us_scr[...]
    usb = us.astype(BF16)
    half = slab_rows // 2
    chunk = slab_rows // n_slabs
    neg_lam = -lam_lru_ref[...]
    softplus = jnp.maximum(neg_lam, 0.0) + jnp.log(1.0 + jnp.exp(-jnp.abs(neg_lam)))
    for jj in range(n_slabs):
        for r0 in (0, half):
            part = _dot(usb[r0:r0 + half], bbd_ref[:, 2 * jj * LANES:(2 * jj + 2) * LANES])
            for j in range(2):
                x_scr[2 * jj + j, r0:r0 + half, :] = part[:, j * LANES:(j + 1) * LANES]
        r0 = jj * chunk
        xc = convb_ref[...]
        for kk in range(CONV_WIDTH):
            lo = lead - (CONV_WIDTH - 1) + kk + r0
            xc = xc + convw_ref[kk:kk + 1, :] * ul_scr[lo:lo + chunk, :]
        xcb = xc.astype(BF16)
        r = _sigmoid(_dot(xcb, wa_ref[...]) + ba_ref[...])
        ig = _sigmoid(_dot(xcb, wx_ref[...]) + bx_ref[...])
        a = jnp.exp(-LRU_C * r * softplus)
        gap = 1.0 - a * a
        bterm = gap * lax.rsqrt(jnp.maximum(gap, TINY)) * (ig * xc)
        for j in range(lru_slabs):
            ab_scr[j, r0:r0 + chunk, :] = a[:, j * LANES:(j + 1) * LANES]
            ab_scr[lru_slabs + j, r0:r0 + chunk, :] = bterm[:, j * LANES:(j + 1) * LANES]
    for b in range(batch):
        ul_scr[lead + b * pitch:lead + b * pitch + SEQ_PAD, :] = (
            ul_scr[lead + (b + 1) * pitch - SEQ_PAD:lead + (b + 1) * pitch, :])

    for c0 in range(0, n_slabs, SCAN_SLABS):
        lam = [(lam_re_ref[:, (c0 + s) * LANES:(c0 + s + 1) * LANES],
                lam_im_ref[:, (c0 + s) * LANES:(c0 + s + 1) * LANES]) for s in range(SCAN_SLABS)]
        with_lru = c0 == 0

        def step(t, carry, c0=c0, lam=lam, with_lru=with_lru):
            new = []
            for s in range(SCAN_SLABS):
                h_re, h_im = carry[2 * s], carry[2 * s + 1]
                lam_re, lam_im = lam[s]
                n_re = lam_re * h_re - lam_im * h_im + x_scr.at[c0 + s][step_rows(t), :]
                n_im = lam_re * h_im + lam_im * h_re + x_scr.at[n_slabs + c0 + s][step_rows(t), :]
                h_scr.at[c0 + s][step_rows(t), :] = n_re
                h_scr.at[n_slabs + c0 + s][step_rows(t), :] = n_im
                new += [n_re, n_im]
            if with_lru:
                for s in range(lru_slabs):
                    h = (ab_scr.at[s][step_rows(t), :] * carry[2 * SCAN_SLABS + s]
                         + ab_scr.at[lru_slabs + s][step_rows(t), :])
                    yl_scr.at[s][step_rows(t), :] = h
                    new.append(h)
            return tuple(new)

        init = []
        for s in range(SCAN_SLABS):
            init += [hstate[c0 + s], hstate[n_slabs + c0 + s]]
        if with_lru:
            init += [lstate[s] for s in range(lru_slabs)]
        final = lax.fori_loop(0, steps, step, tuple(init), unroll=2)
        for s in range(SCAN_SLABS):
            hstate[c0 + s] = final[2 * s]
            hstate[n_slabs + c0 + s] = final[2 * s + 1]
        if with_lru:
            for s in range(lru_slabs):
                lstate[s] = final[2 * SCAN_SLABS + s]

    h_all = jnp.concatenate([h_scr[j].astype(BF16) for j in range(2 * n_slabs)], axis=-1)
    c_all = jnp.concatenate([c_re_ref[...], -c_im_ref[...]], axis=0)
    y = jnp.concatenate([_dot(h_all[r0:r0 + half], c_all) for r0 in (0, half)], axis=0)
    y = _gelu_tanh(y + dskip_ref[...] * us)
    out_scr[0] = y * _sigmoid(_dot(y.astype(BF16), wglu_ref[...]) + bglu_ref[...])
    out_scr[1] = jnp.concatenate([yl_scr[j] for j in range(lru_slabs)], axis=-1)
    for b in range(batch):
        ys_ref[b] = out_scr[0, seq_rows(b), :].astype(ys_ref.dtype)
        yl_ref[b] = out_scr[1, seq_rows(b), :].astype(yl_ref.dtype)


def _recurrent(mix_in, p):
    batch, seq, _ = mix_in.shape
    steps = SCAN_STEPS
    slab_rows = -(-batch * (steps + SEQ_PAD) // LANES) * LANES
    vec = lambda: _const_spec((1, WIDTH))
    sq = lambda: _const_spec((WIDTH, WIDTH))
    out = jax.ShapeDtypeStruct((batch, seq, WIDTH), BF16)
    return pl.pallas_call(
        _scan_body,
        out_shape=(out, out),
        grid=(seq // steps,),
        in_specs=[pl.BlockSpec((batch, steps, WIDTH), lambda t: (0, t, 3)),
                  pl.BlockSpec((batch, steps, WIDTH), lambda t: (0, t, 7)),
                  _const_spec((WIDTH, 2 * SSM_LANES)),
                  _const_spec((batch, SSM_LANES)), _const_spec((batch, SSM_LANES)),
                  _const_spec((SSM_LANES, WIDTH)), _const_spec((SSM_LANES, WIDTH)),
                  vec(), sq(), vec(),
                  _const_spec((CONV_WIDTH, WIDTH)), vec(), sq(), vec(), sq(), vec(), vec()],
        out_specs=(pl.BlockSpec((batch, steps, WIDTH), lambda t: (0, t, 0)),
                   pl.BlockSpec((batch, steps, WIDTH), lambda t: (0, t, 0))),
        scratch_shapes=[pltpu.VMEM((2 * SSM_LANES // LANES, slab_rows, LANES), F32),
                        pltpu.VMEM((2 * SSM_LANES // LANES, slab_rows, LANES), F32),
                        pltpu.VMEM((2 * SSM_LANES // LANES, batch, LANES), F32),
                        pltpu.VMEM((slab_rows, WIDTH), F32),
                        pltpu.VMEM((SUBLANES + slab_rows, WIDTH), F32),
                        pltpu.VMEM((2 * WIDTH // LANES, slab_rows, LANES), F32),
                        pltpu.VMEM((WIDTH // LANES, slab_rows, LANES), F32),
                        pltpu.VMEM((2, slab_rows, WIDTH), F32),
                        pltpu.VMEM((WIDTH // LANES, batch, LANES), F32)],
        compiler_params=_params("arbitrary"),
        name="recurrent",
    )(mix_in, mix_in, p["bbd"], p["lam_re"], p["lam_im"], p["c_re"], p["c_im"], p["d_skip"], p["w_glu"],
      p["b_glu"], p["conv_w"], p["conv_b"], p["w_a"], p["b_a"], p["w_x"], p["b_x"], p["lam"])


def _merge_body(x_ref, pre_g_ref, post_g_ref, y0_ref, y1_ref, y2_ref, y3_ref, wg_ref, wm_ref, wb_ref,
                wo_ref, o_ref):
    x = x_ref[...]
    h = _rms_norm(x, pre_g_ref[...]).astype(BF16)
    y_refs = (y0_ref, y1_ref, y2_ref, y3_ref)
    gates = [_dot(h, wg_ref[:, n * WIDTH:(n + 1) * WIDTH]) for n in range(N_BRANCH)]
    mixes = [(y_refs[n][...].astype(F32) * (gates[n] * _sigmoid(gates[n]))).astype(BF16) for n in range(N_BRANCH)]
    merged = None
    pre = _dot(h, wm_ref[:, 0:D_MODEL])
    for n in range(N_BRANCH):
        branch = _dot(mixes[n], wb_ref[n])
        nxt = _dot(h, wm_ref[:, (n + 1) * D_MODEL:(n + 2) * D_MODEL]) if n + 1 < N_BRANCH else None
        term = _sigmoid(pre) * branch
        merged = term if merged is None else merged + term
        pre = nxt
    out = _dot(merged.astype(BF16), wo_ref[...])
    o_ref[...] = x + _rms_norm(out, post_g_ref[...])


def _merge(x2, pre_g, post_g, ys, w_gate, w_merge, w_branch, w_out):
    rows = x2.shape[0]
    row_spec = lambda w: pl.BlockSpec((ROW_TILE, w), lambda i: (i, 0))
    return pl.pallas_call(
        _merge_body,
        out_shape=jax.ShapeDtypeStruct((rows, D_MODEL), F32),
        grid=(rows // ROW_TILE,),
        in_specs=[row_spec(D_MODEL), _const_spec((1, D_MODEL)), _const_spec((1, D_MODEL)),
                  row_spec(WIDTH), row_spec(WIDTH), row_spec(WIDTH), row_spec(WIDTH),
                  _const_spec((D_MODEL, GATE_COLS)),
                  _const_spec((D_MODEL, N_BRANCH * D_MODEL)),
                  _const_spec((N_BRANCH, WIDTH, D_MODEL)),
                  _const_spec((D_MODEL, D_MODEL))],
        out_specs=row_spec(D_MODEL),
        compiler_params=_params("parallel"),
        name="merge",
    )(x2, pre_g, post_g, *ys, w_gate, w_merge, w_branch, w_out)


def _block_diag(blocks):
    n, r, c = blocks.shape
    eye = jnp.eye(n, dtype=blocks.dtype)
    return (blocks[:, :, None, :] * eye[:, None, :, None]).reshape(n * r, n * c)


def kernel(x, pre_norm_g, post_norm_g, w_in, ssm_a_re, ssm_a_im, ssm_log_dt, ssm_b_re, ssm_b_im, ssm_c_re,
           ssm_c_im, ssm_d, ssm_w_glu, ssm_b_glu, lru_conv_w, lru_conv_b, lru_w_a, lru_b_a, lru_w_x, lru_b_x,
           lru_lambda, w_branch, w_out):
    batch, seq, _ = x.shape
    depth = w_in.shape[0]
    g, n, c = SSM_GROUPS, SSM_STATE, SSM_GROUP
    assert batch % (2 * SUBLANES) == 0 and batch % ATT_BATCH == 0 and batch % RET_BATCH == 0
    assert seq % max(ATT_TILE, RET_TILE, SCAN_STEPS) == 0 and (batch * seq) % ROW_TILE == 0

    tri = jnp.tril(jnp.ones((ATT_TILE, ATT_TILE), F32), -1).astype(BF16)
    ret_tabs = _retention_tables(seq)

    flat = lambda a: a.reshape(depth * g * n, -1)
    log_dt = jnp.broadcast_to(ssm_log_dt[:, :, None], (depth, g, n))
    lam_re, lam_im, bb_re, bb_im = _s5_prep(flat(ssm_a_re), flat(ssm_a_im), flat(log_dt),
                                             flat(ssm_b_re), flat(ssm_b_im))

    rows = batch * seq
    x2 = x.reshape(rows, D_MODEL)
    for l in range(depth):
        w_l = w_in[l].astype(BF16)
        bb = lambda a: _block_diag(jnp.transpose(a.reshape(depth, g, n, c)[l], (0, 2, 1)))
        lam_rows = lambda a: jnp.broadcast_to(a.reshape(depth, 1, g * n)[l], (batch, g * n))
        rec = {
            "bbd": jnp.concatenate([bb(bb_re), bb(bb_im)], axis=1).astype(BF16),
            "lam_re": lam_rows(lam_re), "lam_im": lam_rows(lam_im),
            "c_re": _block_diag(jnp.transpose(ssm_c_re[l], (0, 2, 1))).astype(BF16),
            "c_im": _block_diag(jnp.transpose(ssm_c_im[l], (0, 2, 1))).astype(BF16),
            "d_skip": ssm_d[l][None, :], "w_glu": ssm_w_glu[l].astype(BF16), "b_glu": ssm_b_glu[l][None, :],
            "conv_w": lru_conv_w[l], "conv_b": lru_conv_b[l][None, :],
            "w_a": _block_diag(lru_w_a[l]).astype(BF16), "b_a": lru_b_a[l].reshape(1, WIDTH),
            "w_x": _block_diag(lru_w_x[l]).astype(BF16), "b_x": lru_b_x[l].reshape(1, WIDTH),
            "lam": lru_lambda[l][None, :],
        }
        pre_g = pre_norm_g[l][None, :]
        mix_in = _inproj(x2, pre_g, w_l[:, :MIX_COLS]).reshape(batch, seq, MIX_COLS)
        y_sb = _stick_breaking(mix_in, tri)
        y_ret = _retention(mix_in, ret_tabs)
        y_ssm, y_lru = _recurrent(mix_in, rec)
        ys = tuple(y.reshape(rows, WIDTH) for y in (y_sb, y_ssm, y_ret, y_lru))
        x2 = _merge(x2, pre_g, post_norm_g[l][None, :], ys,
                    w_l[:, MIX_COLS:MIX_COLS + GATE_COLS], w_l[:, MIX_COLS + GATE_COLS:],
                    w_branch[l].astype(BF16), w_out[l].astype(BF16))
    return x2.reshape(batch, seq, D_MODEL)
```

```python
import functools
import math

import jax
import jax.numpy as jnp
from jax import lax
from jax.experimental import pallas as pl
from jax.experimental.pallas import tpu as pltpu

F32 = jnp.float32
BF16 = jnp.bfloat16

D_MODEL = 1024
N_BRANCH = 4
WIDTH = D_MODEL // N_BRANCH
HEAD_DIM = 64
N_HEADS = WIDTH // HEAD_DIM
SSM_GROUPS = 16
SSM_GROUP = 16
SSM_STATE = 64
SSM_LANES = SSM_GROUPS * SSM_STATE
CONV_WIDTH = 4
LRU_C = 8.0
ROPE_BASE = 10000.0
EPS = 1e-6
LOG2_E = math.log2(math.e)
TINY = 1e-30
MIX_COLS = 8 * WIDTH
GATE_COLS = N_BRANCH * WIDTH

LANES = 128
SUBLANES = 8
ROW_TILE = 1024
ATT_TILE = 256
ATT_BATCH = 8
RET_TILE = 256
RET_BATCH = 8
SCAN_STEPS = 64
SEQ_PAD = 4
SCAN_SLABS = 4
VMEM_LIMIT = 56 * 1024 * 1024


def _dot(a, b):
    return jnp.dot(a, b, preferred_element_type=F32)


def _dot_nt(a, b):
    return lax.dot_general(a, b, (((1,), (1,)), ((), ())), preferred_element_type=F32)


def _sigmoid(x):
    return 0.5 * jnp.tanh(0.5 * x) + 0.5


def _rms_norm(x, g):
    return x * lax.rsqrt(jnp.mean(x * x, axis=-1, keepdims=True) + EPS) * g


def _const_spec(shape):
    nd = len(shape)
    return pl.BlockSpec(shape, lambda *_: (0,) * nd, pipeline_mode=pl.Buffered(1))


def _params(*sem):
    return pltpu.CompilerParams(dimension_semantics=sem, vmem_limit_bytes=VMEM_LIMIT)


def _inproj_body(x_ref, g_ref, w_ref, o_ref):
    h = _rms_norm(x_ref[...], g_ref[...]).astype(BF16)
    o_ref[...] = _dot(h, w_ref[...]).astype(o_ref.dtype)


def _inproj(x2, g, w_l):
    rows = x2.shape[0]
    return pl.pallas_call(
        _inproj_body,
        out_shape=jax.ShapeDtypeStruct((rows, MIX_COLS), BF16),
        grid=(rows // ROW_TILE,),
        in_specs=[pl.BlockSpec((ROW_TILE, D_MODEL), lambda i: (i, 0)),
                  _const_spec((1, D_MODEL)),
                  _const_spec((D_MODEL, MIX_COLS))],
        out_specs=pl.BlockSpec((ROW_TILE, MIX_COLS), lambda i: (i, 0)),
        compiler_params=_params("parallel"),
        name="inproj",
    )(x2, g, w_l)


def _sb_body(q_ref, k_ref, v_ref, tri_ref, o_ref, acc_ref):
    i = pl.program_id(1)
    nb, tq, _ = q_ref.shape
    lane_head = lax.broadcasted_iota(jnp.int32, (1, WIDTH), 1) // HEAD_DIM
    row = lax.broadcasted_iota(jnp.int32, (tq, tq), 0)
    col = lax.broadcasted_iota(jnp.int32, (tq, tq), 1)
    causal = col < row
    tri = tri_ref[...]
    acc_ref[...] = jnp.zeros_like(acc_ref)

    zero = jnp.zeros((), BF16)
    head_masks = [lane_head == h for h in range(N_HEADS)]
    z_scale = LOG2_E * HEAD_DIM ** -0.5
    q_heads = [[jnp.where(m, q_ref[n], zero) for m in head_masks] for n in range(nb)]

    def key_tile(j, carries, diagonal):
        k0 = pl.multiple_of(j * tq, tq)
        chains = [(n, h) for n in range(nb) for h in range(N_HEADS)]
        kj = [k_ref[n, pl.ds(k0, tq), :] for n in range(nb)]
        vj = [v_ref[n, pl.ds(k0, tq), :] for n in range(nb)]
        nc = len(chains)
        scores, fails, biases, afters, parts = {}, {}, {}, {}, {}
        for slot in range(nc + 2):
            if slot < nc:
                n, h = chains[slot]
                scores[slot] = _dot_nt(q_heads[n][h], kj[n])
            c = slot - 1
            if 0 <= c < nc:
                z = scores.pop(c) * z_scale
                neg = jnp.minimum(z, 0.0)
                pos = z - neg
                soft = jnp.log2(1.0 + jnp.exp2(neg - pos))
                fail = pos + soft
                if diagonal:
                    fail = jnp.where(causal, fail, 0.0)
                fails[c] = fail.astype(BF16)
                biases[c] = (neg - soft) - carries[c]
                afters[c] = _dot(fails[c], tri)
            c = slot - 2
            if 0 <= c < nc:
                n, h = chains[c]
                w = jnp.exp2(biases.pop(c) - afters[c])
                if diagonal:
                    w = jnp.where(causal, w, 0.0)
                parts[c] = _dot(w.astype(BF16), jnp.where(head_masks[h], vj[n], zero))
        for n in range(nb):
            out = parts[n * N_HEADS]
            for h in range(1, N_HEADS):
                out = out + parts[n * N_HEADS + h]
            acc_ref[n] += out
        return tuple(carry + (afters[c][:, 0:1] + fails[c][:, 0:1].astype(F32)) for c, carry in enumerate(carries))

    carries = key_tile(i, (jnp.zeros((tq, 1), F32),) * (nb * N_HEADS), True)
    lax.fori_loop(0, i, lambda jj, c: key_tile(i - 1 - jj, c, False), carries)
    o_ref[...] = acc_ref[...].astype(o_ref.dtype)


def _stick_breaking(mix_in, tri):
    batch, seq, _ = mix_in.shape
    nb = ATT_BATCH
    return pl.pallas_call(
        _sb_body,
        out_shape=jax.ShapeDtypeStruct((batch, seq, WIDTH), BF16),
        grid=(batch // nb, seq // ATT_TILE),
        in_specs=[pl.BlockSpec((nb, ATT_TILE, WIDTH), lambda b, i: (b, i, 0)),
                  pl.BlockSpec((nb, seq, WIDTH), lambda b, i: (b, 0, 1)),
                  pl.BlockSpec((nb, seq, WIDTH), lambda b, i: (b, 0, 2)),
                  _const_spec((ATT_TILE, ATT_TILE))],
        out_specs=pl.BlockSpec((nb, ATT_TILE, WIDTH), lambda b, i: (b, i, 0)),
        scratch_shapes=[pltpu.VMEM((nb, ATT_TILE, WIDTH), F32)],
        compiler_params=_params("parallel", "arbitrary"),
        name="stick_breaking",
    )(mix_in, mix_in, mix_in, tri)


def _ret_body(q_ref, k_ref, v_ref, cos_ref, sin_ref, rot_ref, decay_ref, xi_ref, zeta_ref, cd_ref,
              bd_ref, avg_ref, o_ref, state_ref):
    @pl.when(pl.program_id(1) == 0)
    def _():
        state_ref[...] = jnp.zeros_like(state_ref)

    cosf = cos_ref[...]
    sinf = sin_ref[...]
    rot = rot_ref[...]
    avg = avg_ref[...]
    lane_head = lax.broadcasted_iota(jnp.int32, (1, WIDTH), 1) // HEAD_DIM
    zero = jnp.zeros((), BF16)
    seqs = range(q_ref.shape[0])
    qs = [q_ref[n] for n in seqs]
    ks = [k_ref[n] for n in seqs]
    vs = [v_ref[n] for n in seqs]
    q_rot = [_dot(qs[n], rot) for n in seqs]
    k_rot = [_dot(ks[n], rot) for n in seqs]
    qr = [qs[n].astype(F32) * cosf + q_rot[n] * sinf for n in seqs]
    kr = [(ks[n].astype(F32) * cosf + k_rot[n] * sinf) * (HEAD_DIM ** -0.5) for n in seqs]
    qrb = [x.astype(BF16) for x in qr]
    krb = [x.astype(BF16) for x in kr]
    head_masks = [lane_head == h for h in range(N_HEADS)]
    scores = [[_dot_nt(jnp.where(m, qrb[n], zero), krb[n]) for m in head_masks] for n in seqs]
    states = [state_ref[n] for n in seqs]
    cross = [_dot((qr[n] * xi_ref[...]).astype(BF16), states[n].astype(BF16)) for n in seqs]
    kv = [_dot((kr[n] * zeta_ref[...]).T.astype(BF16), vs[n]) for n in seqs]
    outs = []
    for n in seqs:
        o = cross[n]
        for h, m in enumerate(head_masks):
            o = o + _dot((scores[n][h] * decay_ref[h]).astype(BF16), jnp.where(m, vs[n], zero))
        outs.append(o)
        state_ref[n] = cd_ref[...] * states[n] + kv[n] * bd_ref[...]
    means = [_dot(o.astype(BF16), avg) for o in outs]
    devs = [outs[n] - means[n] for n in seqs]
    variances = [_dot((d * d).astype(BF16), avg) for d in devs]
    for n in seqs:
        o_ref[n] = (devs[n] * lax.rsqrt(variances[n] + EPS)).astype(o_ref.dtype)


def _retention(mix_in, tabs):
    batch, seq, _ = mix_in.shape
    c = RET_TILE
    nb = RET_BATCH
    tile = lambda col: pl.BlockSpec((nb, c, WIDTH), lambda b, n: (b, n, col))
    return pl.pallas_call(
        _ret_body,
        out_shape=jax.ShapeDtypeStruct((batch, seq, WIDTH), BF16),
        grid=(batch // nb, seq // c),
        in_specs=[tile(4), tile(5), tile(6),
                  pl.BlockSpec((c, WIDTH), lambda b, n: (n, 0)),
                  pl.BlockSpec((c, WIDTH), lambda b, n: (n, 0)),
                  _const_spec((WIDTH, WIDTH)),
                  _const_spec((N_HEADS, c, c)),
                  _const_spec((c, WIDTH)),
                  _const_spec((c, WIDTH)),
                  _const_spec((WIDTH, 1)),
                  _const_spec((WIDTH, WIDTH)),
                  _const_spec((WIDTH, WIDTH))],
        out_specs=pl.BlockSpec((nb, c, WIDTH), lambda b, n: (b, n, 0)),
        scratch_shapes=[pltpu.VMEM((nb, WIDTH, WIDTH), F32)],
        compiler_params=_params("parallel", "arbitrary"),
        name="retention",
    )(mix_in, mix_in, mix_in, tabs["cos"], tabs["sin"], tabs["rot"], tabs["decay"], tabs["xi"],
      tabs["zeta"], tabs["cd"], tabs["bd"], tabs["avg"])


def _retention_tables(seq):
    c = RET_TILE
    half = HEAD_DIM // 2
    inv_freq = ROPE_BASE ** (-jnp.arange(half, dtype=F32) / half)
    ang = jnp.arange(seq, dtype=F32)[:, None] * inv_freq[None, :]
    cos = jnp.tile(jnp.cos(ang), (1, 2 * N_HEADS))
    sin = jnp.tile(jnp.sin(ang), (1, 2 * N_HEADS))
    lane = jnp.arange(WIDTH)
    d = lane % HEAD_DIM
    src = jnp.where(d < half, lane + half, lane - half)
    sign = jnp.where(d < half, -1.0, 1.0)
    rot = jnp.where(lane[:, None] == src[None, :], sign[None, :], 0.0).astype(BF16)
    log_g = jnp.log1p(-(2.0 ** (-5.0 - jnp.arange(N_HEADS, dtype=F32))))
    i = jnp.arange(c, dtype=F32)
    rel = i[:, None] - i[None, :]
    decay = jnp.where(rel >= 0, jnp.exp(log_g[:, None, None] * jnp.maximum(rel, 0.0)), 0.0)
    log_g_lane = log_g[lane // HEAD_DIM]
    xi = jnp.exp(log_g_lane[None, :] * (i + 1.0)[:, None])
    zeta = jnp.exp(log_g_lane[None, :] * (c - 1 - i)[:, None])
    cd = jnp.exp(log_g_lane * c)[:, None]
    same_head = (lane[:, None] // HEAD_DIM) == (lane[None, :] // HEAD_DIM)
    return {"cos": cos, "sin": sin, "rot": rot, "decay": decay, "xi": xi, "zeta": zeta, "cd": cd,
            "bd": same_head.astype(F32), "avg": (same_head.astype(F32) / HEAD_DIM).astype(BF16)}


def _s5_prep_body(a_re_ref, a_im_ref, log_dt_ref, b_re_ref, b_im_ref, lam_re_ref, lam_im_ref,
                  bb_re_ref, bb_im_ref):
    a_re = a_re_ref[...]
    a_im = a_im_ref[...]
    dt = jnp.exp(log_dt_ref[...])
    mag = jnp.exp(dt * a_re)
    ab_re = mag * jnp.cos(dt * a_im)
    ab_im = mag * jnp.sin(dt * a_im)
    den = a_re * a_re + a_im * a_im
    num_re = ab_re - 1.0
    f_re = (num_re * a_re + ab_im * a_im) / den
    f_im = (ab_im * a_re - num_re * a_im) / den
    b_re = b_re_ref[...]
    b_im = b_im_ref[...]
    lam_re_ref[...] = ab_re
    lam_im_ref[...] = ab_im
    bb_re_ref[...] = f_re * b_re - f_im * b_im
    bb_im_ref[...] = f_re * b_im + f_im * b_re


def _s5_prep(a_re, a_im, log_dt, b_re, b_im):
    rows = a_re.shape[0]
    col = jax.ShapeDtypeStruct((rows, 1), F32)
    mat = jax.ShapeDtypeStruct((rows, SSM_GROUP), F32)
    return pl.pallas_call(_s5_prep_body, out_shape=(col, col, mat, mat), name="s5_prep")(
        a_re, a_im, log_dt, b_re, b_im)


def _gelu_tanh(y):
    return 0.5 * y * (1.0 + jnp.tanh(math.sqrt(2.0 / math.pi) * (y + 0.044715 * (y * y * y))))


def _scan_body(us_ref, ul_ref, bbd_ref, lam_re_ref, lam_im_ref, c_re_ref, c_im_ref, dskip_ref, wglu_ref,
               bglu_ref, convw_ref, convb_ref, wa_ref, ba_ref, wx_ref, bx_ref, lam_lru_ref,
               ys_ref, yl_ref,
               x_scr, h_scr, hstate, us_scr, ul_scr, ab_scr, yl_scr, out_scr, lstate):
    batch, steps, _ = us_ref.shape
    n_slabs = SSM_LANES // LANES
    lru_slabs = WIDTH // LANES
    pitch = steps + SEQ_PAD
    slab_rows = x_scr.shape[1]
    lead = SUBLANES
    step_rows = lambda t: pl.ds(SEQ_PAD + t, batch, stride=pitch)
    seq_rows = lambda b: slice(b * pitch + SEQ_PAD, (b + 1) * pitch)

    @pl.when(pl.program_id(0) == 0)
    def _():
        hstate[...] = jnp.zeros_like(hstate)
        lstate[...] = jnp.zeros_like(lstate)
        us_scr[...] = jnp.zeros_like(us_scr)
        ul_scr[...] = jnp.zeros_like(ul_scr)
        h_scr[...] = jnp.zeros_like(h_scr)
        yl_scr[...] = jnp.zeros_like(yl_scr)

    for b in range(batch):
        us_scr[seq_rows(b), :] = us_ref[b].astype(F32)
        ul_scr[lead + b * pitch + SEQ_PAD:lead + (b + 1) * pitch, :] = ul_ref[b].astype(F32)
    us = us_scr[...]
    usb = us.astype(BF16)
    half = slab_rows // 2
    chunk = slab_rows // n_slabs
    neg_lam = -lam_lru_ref[...]
    softplus = jnp.maximum(neg_lam, 0.0) + jnp.log(1.0 + jnp.exp(-jnp.abs(neg_lam)))
    for jj in range(n_slabs):
        for r0 in (0, half):
            part = _dot(usb[r0:r0 + half], bbd_ref[:, 2 * jj * LANES:(2 * jj + 2) * LANES])
            for j in range(2):
                x_scr[2 * jj + j, r0:r0 + half, :] = part[:, j * LANES:(j + 1) * LANES]
        r0 = jj * chunk
        xc = convb_ref[...]
        for kk in range(CONV_WIDTH):
            lo = lead - (CONV_WIDTH - 1) + kk + r0
            xc = xc + convw_ref[kk:kk + 1, :] * ul_scr[lo:lo + chunk, :]
        xcb = xc.astype(BF16)
        r = _sigmoid(_dot(xcb, wa_ref[...]) + ba_ref[...])
        ig = _sigmoid(_dot(xcb, wx_ref[...]) + bx_ref[...])
        a = jnp.exp(-LRU_C * r * softplus)
        gap = 1.0 - a * a
        bterm = gap * lax.rsqrt(jnp.maximum(gap, TINY)) * (ig * xc)
        for j in range(lru_slabs):
            ab_scr[j, r0:r0 + chunk, :] = a[:, j * LANES:(j + 1) * LANES]
            ab_scr[lru_slabs + j, r0:r0 + chunk, :] = bterm[:, j * LANES:(j + 1) * LANES]
    for b in range(batch):
        ul_scr[lead + b * pitch:lead + b * pitch + SEQ_PAD, :] = (
            ul_scr[lead + (b + 1) * pitch - SEQ_PAD:lead + (b + 1) * pitch, :])

    for c0 in range(0, n_slabs, SCAN_SLABS):
        lam = [(lam_re_ref[:, (c0 + s) * LANES:(c0 + s + 1) * LANES],
                lam_im_ref[:, (c0 + s) * LANES:(c0 + s + 1) * LANES]) for s in range(SCAN_SLABS)]
        with_lru = c0 == 0

        def step(t, carry, c0=c0, lam=lam, with_lru=with_lru):
            new = []
            for s in range(SCAN_SLABS):
                h_re, h_im = carry[2 * s], carry[2 * s + 1]
                lam_re, lam_im = lam[s]
                n_re = lam_re * h_re - lam_im * h_im + x_scr.at[c0 + s][step_rows(t), :]
                n_im = lam_re * h_im + lam_im * h_re + x_scr.at[n_slabs + c0 + s][step_rows(t), :]
                h_scr.at[c0 + s][step_rows(t), :] = n_re
                h_scr.at[n_slabs + c0 + s][step_rows(t), :] = n_im
                new += [n_re, n_im]
            if with_lru:
                for s in range(lru_slabs):
                    h = (ab_scr.at[s][step_rows(t), :] * carry[2 * SCAN_SLABS + s]
                         + ab_scr.at[lru_slabs + s][step_rows(t), :])
                    yl_scr.at[s][step_rows(t), :] = h
                    new.append(h)
            return tuple(new)

        init = []
        for s in range(SCAN_SLABS):
            init += [hstate[c0 + s], hstate[n_slabs + c0 + s]]
        if with_lru:
            init += [lstate[s] for s in range(lru_slabs)]
        final = lax.fori_loop(0, steps, step, tuple(init), unroll=2)
        for s in range(SCAN_SLABS):
            hstate[c0 + s] = final[2 * s]
            hstate[n_slabs + c0 + s] = final[2 * s + 1]
        if with_lru:
            for s in range(lru_slabs):
                lstate[s] = final[2 * SCAN_SLABS + s]

    h_all = jnp.concatenate([h_scr[j].astype(BF16) for j in range(2 * n_slabs)], axis=-1)
    c_all = jnp.concatenate([c_re_ref[...], -c_im_ref[...]], axis=0)
    y = jnp.concatenate([_dot(h_all[r0:r0 + half], c_all) for r0 in (0, half)], axis=0)
    y = _gelu_tanh(y + dskip_ref[...] * us)
    out_scr[0] = y * _sigmoid(_dot(y.astype(BF16), wglu_ref[...]) + bglu_ref[...])
    out_scr[1] = jnp.concatenate([yl_scr[j] for j in range(lru_slabs)], axis=-1)
    for b in range(batch):
        ys_ref[b] = out_scr[0, seq_rows(b), :].astype(ys_ref.dtype)
        yl_ref[b] = out_scr[1, seq_rows(b), :].astype(yl_ref.dtype)


def _recurrent(mix_in, p):
    batch, seq, _ = mix_in.shape
    steps = SCAN_STEPS
    slab_rows = -(-batch * (steps + SEQ_PAD) // LANES) * LANES
    vec = lambda: _const_spec((1, WIDTH))
    sq = lambda: _const_spec((WIDTH, WIDTH))
    out = jax.ShapeDtypeStruct((batch, seq, WIDTH), BF16)
    return pl.pallas_call(
        _scan_body,
        out_shape=(out, out),
        grid=(seq // steps,),
        in_specs=[pl.BlockSpec((batch, steps, WIDTH), lambda t: (0, t, 3)),
                  pl.BlockSpec((batch, steps, WIDTH), lambda t: (0, t, 7)),
                  _const_spec((WIDTH, 2 * SSM_LANES)),
                  _const_spec((batch, SSM_LANES)), _const_spec((batch, SSM_LANES)),
                  _const_spec((SSM_LANES, WIDTH)), _const_spec((SSM_LANES, WIDTH)),
                  vec(), sq(), vec(),
                  _const_spec((CONV_WIDTH, WIDTH)), vec(), sq(), vec(), sq(), vec(), vec()],
        out_specs=(pl.BlockSpec((batch, steps, WIDTH), lambda t: (0, t, 0)),
                   pl.BlockSpec((batch, steps, WIDTH), lambda t: (0, t, 0))),
        scratch_shapes=[pltpu.VMEM((2 * SSM_LANES // LANES, slab_rows, LANES), F32),
                        pltpu.VMEM((2 * SSM_LANES // LANES, slab_rows, LANES), F32),
                        pltpu.VMEM((2 * SSM_LANES // LANES, batch, LANES), F32),
                        pltpu.VMEM((slab_rows, WIDTH), F32),
                        pltpu.VMEM((SUBLANES + slab_rows, WIDTH), F32),
                        pltpu.VMEM((2 * WIDTH // LANES, slab_rows, LANES), F32),
                        pltpu.VMEM((WIDTH // LANES, slab_rows, LANES), F32),
                        pltpu.VMEM((2, slab_rows, WIDTH), F32),
                        pltpu.VMEM((WIDTH // LANES, batch, LANES), F32)],
        compiler_params=_params("arbitrary"),
        name="recurrent",
    )(mix_in, mix_in, p["bbd"], p["lam_re"], p["lam_im"], p["c_re"], p["c_im"], p["d_skip"], p["w_glu"],
      p["b_glu"], p["conv_w"], p["conv_b"], p["w_a"], p["b_a"], p["w_x"], p["b_x"], p["lam"])


def _merge_body(x_ref, pre_g_ref, post_g_ref, y0_ref, y1_ref, y2_ref, y3_ref, wg_ref, wm0_ref, wm1_ref, wm2_ref,
                wm3_ref, wb_ref, wo_ref, o_ref):
    x = x_ref[...]
    h = _rms_norm(x, pre_g_ref[...]).astype(BF16)
    y_refs = (y0_ref, y1_ref, y2_ref, y3_ref)
    wm_refs = (wm0_ref, wm1_ref, wm2_ref, wm3_ref)
    gates = [_dot(h, wg_ref[:, n * WIDTH:(n + 1) * WIDTH]) for n in range(N_BRANCH)]
    mixes = [(y_refs[n][...].astype(F32) * (gates[n] * _sigmoid(gates[n]))).astype(BF16) for n in range(N_BRANCH)]
    merged = None
    pre = _dot(h, wm_refs[0][...])
    for n in range(N_BRANCH):
        branch = _dot(mixes[n], wb_ref[n])
        nxt = _dot(h, wm_refs[n + 1][...]) if n + 1 < N_BRANCH else None
        term = _sigmoid(pre) * branch
        merged = term if merged is None else merged + term
        pre = nxt
    out = _dot(merged.astype(BF16), wo_ref[...])
    o_ref[...] = x + _rms_norm(out, post_g_ref[...])


def _merge(x2, pre_g, post_g, ys, w_l, w_branch, w_out):
    assert GATE_COLS == D_MODEL and MIX_COLS % D_MODEL == 0
    rows = x2.shape[0]
    row_spec = lambda w: pl.BlockSpec((ROW_TILE, w), lambda i: (i, 0))
    col_block = lambda j: pl.BlockSpec((D_MODEL, D_MODEL), lambda i: (0, j), pipeline_mode=pl.Buffered(1))
    first = MIX_COLS // D_MODEL
    return pl.pallas_call(
        _merge_body,
        out_shape=jax.ShapeDtypeStruct((rows, D_MODEL), F32),
        grid=(rows // ROW_TILE,),
        in_specs=[row_spec(D_MODEL), _const_spec((1, D_MODEL)), _const_spec((1, D_MODEL)),
                  row_spec(WIDTH), row_spec(WIDTH), row_spec(WIDTH), row_spec(WIDTH),
                  col_block(first),
                  *[col_block(first + 1 + n) for n in range(N_BRANCH)],
                  _const_spec((N_BRANCH, WIDTH, D_MODEL)),
                  _const_spec((D_MODEL, D_MODEL))],
        out_specs=row_spec(D_MODEL),
        compiler_params=_params("parallel"),
        name="merge",
    )(x2, pre_g, post_g, *ys, *([w_l] * (1 + N_BRANCH)), w_branch, w_out)


def _block_diag(blocks):
    n, r, c = blocks.shape
    eye = jnp.eye(n, dtype=blocks.dtype)
    return (blocks[:, :, None, :] * eye[:, None, :, None]).reshape(n * r, n * c)


def kernel(x, pre_norm_g, post_norm_g, w_in, ssm_a_re, ssm_a_im, ssm_log_dt, ssm_b_re, ssm_b_im, ssm_c_re,
           ssm_c_im, ssm_d, ssm_w_glu, ssm_b_glu, lru_conv_w, lru_conv_b, lru_w_a, lru_b_a, lru_w_x, lru_b_x,
           lru_lambda, w_branch, w_out):
    batch, seq, _ = x.shape
    depth = w_in.shape[0]
    g, n, c = SSM_GROUPS, SSM_STATE, SSM_GROUP
    assert batch % (2 * SUBLANES) == 0 and batch % ATT_BATCH == 0 and batch % RET_BATCH == 0
    assert seq % max(ATT_TILE, RET_TILE, SCAN_STEPS) == 0 and (batch * seq) % ROW_TILE == 0

    tri = jnp.tril(jnp.ones((ATT_TILE, ATT_TILE), F32), -1).astype(BF16)
    ret_tabs = _retention_tables(seq)

    flat = lambda a: a.reshape(depth * g * n, -1)
    log_dt = jnp.broadcast_to(ssm_log_dt[:, :, None], (depth, g, n))
    lam_re, lam_im, bb_re, bb_im = _s5_prep(flat(ssm_a_re), flat(ssm_a_im), flat(log_dt),
                                             flat(ssm_b_re), flat(ssm_b_im))

    rows = batch * seq
    x2 = x.reshape(rows, D_MODEL)
    for l in range(depth):
        w_l = w_in[l].astype(BF16)
        bb = lambda a: _block_diag(jnp.transpose(a.reshape(depth, g, n, c)[l], (0, 2, 1)))
        lam_rows = lambda a: jnp.broadcast_to(a.reshape(depth, 1, g * n)[l], (batch, g * n))
        rec = {
            "bbd": jnp.concatenate([bb(bb_re), bb(bb_im)], axis=1).astype(BF16),
            "lam_re": lam_rows(lam_re), "lam_im": lam_rows(lam_im),
            "c_re": _block_diag(jnp.transpose(ssm_c_re[l], (0, 2, 1))).astype(BF16),
            "c_im": _block_diag(jnp.transpose(ssm_c_im[l], (0, 2, 1))).astype(BF16),
            "d_skip": ssm_d[l][None, :], "w_glu": ssm_w_glu[l].astype(BF16), "b_glu": ssm_b_glu[l][None, :],
            "conv_w": lru_conv_w[l], "conv_b": lru_conv_b[l][None, :],
            "w_a": _block_diag(lru_w_a[l]).astype(BF16), "b_a": lru_b_a[l].reshape(1, WIDTH),
            "w_x": _block_diag(lru_w_x[l]).astype(BF16), "b_x": lru_b_x[l].reshape(1, WIDTH),
            "lam": lru_lambda[l][None, :],
        }
        pre_g = pre_norm_g[l][None, :]
        mix_in = _inproj(x2, pre_g, w_l).reshape(batch, seq, MIX_COLS)
        y_sb = _stick_breaking(mix_in, tri)
        y_ret = _retention(mix_in, ret_tabs)
        y_ssm, y_lru = _recurrent(mix_in, rec)
        ys = tuple(y.reshape(rows, WIDTH) for y in (y_sb, y_ssm, y_ret, y_lru))
        x2 = _merge(x2, pre_g, post_norm_g[l][None, :], ys, w_l, w_branch[l].astype(BF16), w_out[l].astype(BF16))
    return x2.reshape(batch, seq, D_MODEL)
```

```python
import functools
import math

import jax
import jax.numpy as jnp
from jax import lax
from jax.experimental import pallas as pl
from jax.experimental.pallas import tpu as pltpu

F32 = jnp.float32
BF16 = jnp.bfloat16

D_MODEL = 1024
N_BRANCH = 4
WIDTH = D_MODEL // N_BRANCH
HEAD_DIM = 64
N_HEADS = WIDTH // HEAD_DIM
SSM_GROUPS = 16
SSM_GROUP = 16
SSM_STATE = 64
SSM_LANES = SSM_GROUPS * SSM_STATE
CONV_WIDTH = 4
LRU_C = 8.0
ROPE_BASE = 10000.0
EPS = 1e-6
LOG2_E = math.log2(math.e)
TINY = 1e-30
MIX_COLS = 8 * WIDTH
GATE_COLS = N_BRANCH * WIDTH

LANES = 128
SUBLANES = 8
ROW_TILE = 1024
ATT_TILE = 256
ATT_BATCH = 8
RET_TILE = 256
RET_BATCH = 8
SCAN_STEPS = 64
SEQ_PAD = 4
CONV_LEAD = 16
CONV_WINDOW = 256
SCAN_SLABS = 4
VMEM_LIMIT = 56 * 1024 * 1024


def _dot(a, b):
    return jnp.dot(a, b, preferred_element_type=F32)


def _dot_nt(a, b):
    return lax.dot_general(a, b, (((1,), (1,)), ((), ())), preferred_element_type=F32)


def _sigmoid(x):
    return 0.5 * jnp.tanh(0.5 * x) + 0.5


def _rms_norm(x, g):
    return x * lax.rsqrt(jnp.mean(x * x, axis=-1, keepdims=True) + EPS) * g


def _const_spec(shape):
    nd = len(shape)
    return pl.BlockSpec(shape, lambda *_: (0,) * nd, pipeline_mode=pl.Buffered(1))


def _params(*sem):
    return pltpu.CompilerParams(dimension_semantics=sem, vmem_limit_bytes=VMEM_LIMIT)


def _inproj_body(x_ref, g_ref, w_ref, o_ref):
    h = _rms_norm(x_ref[...], g_ref[...]).astype(BF16)
    o_ref[...] = _dot(h, w_ref[...]).astype(o_ref.dtype)


def _inproj(x2, g, w_l):
    rows = x2.shape[0]
    return pl.pallas_call(
        _inproj_body,
        out_shape=jax.ShapeDtypeStruct((rows, MIX_COLS), BF16),
        grid=(rows // ROW_TILE,),
        in_specs=[pl.BlockSpec((ROW_TILE, D_MODEL), lambda i: (i, 0)),
                  _const_spec((1, D_MODEL)),
                  _const_spec((D_MODEL, MIX_COLS))],
        out_specs=pl.BlockSpec((ROW_TILE, MIX_COLS), lambda i: (i, 0)),
        compiler_params=_params("parallel"),
        name="inproj",
    )(x2, g, w_l)


def _sb_body(q_ref, k_ref, v_ref, tri_ref, o_ref, acc_ref):
    i = pl.program_id(1)
    nb, tq, _ = q_ref.shape
    lane_head = lax.broadcasted_iota(jnp.int32, (1, WIDTH), 1) // HEAD_DIM
    row = lax.broadcasted_iota(jnp.int32, (tq, tq), 0)
    col = lax.broadcasted_iota(jnp.int32, (tq, tq), 1)
    causal = col < row
    tri = tri_ref[...]
    acc_ref[...] = jnp.zeros_like(acc_ref)

    zero = jnp.zeros((), BF16)
    head_masks = [lane_head == h for h in range(N_HEADS)]
    z_scale = LOG2_E * HEAD_DIM ** -0.5
    q_heads = [[jnp.where(m, q_ref[n], zero) for m in head_masks] for n in range(nb)]

    def key_tile(j, carries, diagonal):
        k0 = pl.multiple_of(j * tq, tq)
        chains = [(n, h) for n in range(nb) for h in range(N_HEADS)]
        kj = [k_ref[n, pl.ds(k0, tq), :] for n in range(nb)]
        vj = [v_ref[n, pl.ds(k0, tq), :] for n in range(nb)]
        nc = len(chains)
        scores, fails, biases, afters, parts = {}, {}, {}, {}, {}
        for slot in range(nc + 2):
            if slot < nc:
                n, h = chains[slot]
                scores[slot] = _dot_nt(q_heads[n][h], kj[n])
            c = slot - 1
            if 0 <= c < nc:
                z = scores.pop(c) * z_scale
                neg = jnp.minimum(z, 0.0)
                pos = z - neg
                soft = jnp.log2(1.0 + jnp.exp2(neg - pos))
                fail = pos + soft
                if diagonal:
                    fail = jnp.where(causal, fail, 0.0)
                fails[c] = fail.astype(BF16)
                biases[c] = (neg - soft) - carries[c]
                afters[c] = _dot(fails[c], tri)
            c = slot - 2
            if 0 <= c < nc:
                n, h = chains[c]
                w = jnp.exp2(biases.pop(c) - afters[c])
                if diagonal:
                    w = jnp.where(causal, w, 0.0)
                parts[c] = _dot(w.astype(BF16), jnp.where(head_masks[h], vj[n], zero))
        for n in range(nb):
            out = parts[n * N_HEADS]
            for h in range(1, N_HEADS):
                out = out + parts[n * N_HEADS + h]
            acc_ref[n] += out
        return tuple(carry + (afters[c][:, 0:1] + fails[c][:, 0:1].astype(F32)) for c, carry in enumerate(carries))

    carries = key_tile(i, (jnp.zeros((tq, 1), F32),) * (nb * N_HEADS), True)
    lax.fori_loop(0, i, lambda jj, c: key_tile(i - 1 - jj, c, False), carries)
    o_ref[...] = acc_ref[...].astype(o_ref.dtype)


def _stick_breaking(mix_in, tri):
    batch, seq, _ = mix_in.shape
    nb = ATT_BATCH
    return pl.pallas_call(
        _sb_body,
        out_shape=jax.ShapeDtypeStruct((batch, seq, WIDTH), BF16),
        grid=(batch // nb, seq // ATT_TILE),
        in_specs=[pl.BlockSpec((nb, ATT_TILE, WIDTH), lambda b, i: (b, i, 0)),
                  pl.BlockSpec((nb, seq, WIDTH), lambda b, i: (b, 0, 1)),
                  pl.BlockSpec((nb, seq, WIDTH), lambda b, i: (b, 0, 2)),
                  _const_spec((ATT_TILE, ATT_TILE))],
        out_specs=pl.BlockSpec((nb, ATT_TILE, WIDTH), lambda b, i: (b, i, 0)),
        scratch_shapes=[pltpu.VMEM((nb, ATT_TILE, WIDTH), F32)],
        compiler_params=_params("parallel", "arbitrary"),
        name="stick_breaking",
    )(mix_in, mix_in, mix_in, tri)


def _ret_body(q_ref, k_ref, v_ref, cos_ref, sin_ref, rot_ref, decay_ref, xi_ref, zeta_ref, cd_ref,
              bd_ref, avg_ref, o_ref, state_ref):
    @pl.when(pl.program_id(1) == 0)
    def _():
        state_ref[...] = jnp.zeros_like(state_ref)

    cosf = cos_ref[...]
    sinf = sin_ref[...]
    rot = rot_ref[...]
    avg = avg_ref[...]
    lane_head = lax.broadcasted_iota(jnp.int32, (1, WIDTH), 1) // HEAD_DIM
    zero = jnp.zeros((), BF16)
    seqs = range(q_ref.shape[0])
    qs = [q_ref[n] for n in seqs]
    ks = [k_ref[n] for n in seqs]
    vs = [v_ref[n] for n in seqs]
    q_rot = [_dot(qs[n], rot) for n in seqs]
    k_rot = [_dot(ks[n], rot) for n in seqs]
    qr = [qs[n].astype(F32) * cosf + q_rot[n] * sinf for n in seqs]
    kr = [(ks[n].astype(F32) * cosf + k_rot[n] * sinf) * (HEAD_DIM ** -0.5) for n in seqs]
    qrb = [x.astype(BF16) for x in qr]
    krb = [x.astype(BF16) for x in kr]
    head_masks = [lane_head == h for h in range(N_HEADS)]
    scores = [[_dot_nt(jnp.where(m, qrb[n], zero), krb[n]) for m in head_masks] for n in seqs]
    states = [state_ref[n] for n in seqs]
    cross = [_dot((qr[n] * xi_ref[...]).astype(BF16), states[n].astype(BF16)) for n in seqs]
    kv = [_dot((kr[n] * zeta_ref[...]).T.astype(BF16), vs[n]) for n in seqs]
    outs = []
    for n in seqs:
        o = cross[n]
        for h, m in enumerate(head_masks):
            o = o + _dot((scores[n][h] * decay_ref[h]).astype(BF16), jnp.where(m, vs[n], zero))
        outs.append(o)
        state_ref[n] = cd_ref[...] * states[n] + kv[n] * bd_ref[...]
    means = [_dot(o.astype(BF16), avg) for o in outs]
    devs = [outs[n] - means[n] for n in seqs]
    variances = [_dot((d * d).astype(BF16), avg) for d in devs]
    for n in seqs:
        o_ref[n] = (devs[n] * lax.rsqrt(variances[n] + EPS)).astype(o_ref.dtype)


def _retention(mix_in, tabs):
    batch, seq, _ = mix_in.shape
    c = RET_TILE
    nb = RET_BATCH
    tile = lambda col: pl.BlockSpec((nb, c, WIDTH), lambda b, n: (b, n, col))
    return pl.pallas_call(
        _ret_body,
        out_shape=jax.ShapeDtypeStruct((batch, seq, WIDTH), BF16),
        grid=(batch // nb, seq // c),
        in_specs=[tile(4), tile(5), tile(6),
                  pl.BlockSpec((c, WIDTH), lambda b, n: (n, 0)),
                  pl.BlockSpec((c, WIDTH), lambda b, n: (n, 0)),
                  _const_spec((WIDTH, WIDTH)),
                  _const_spec((N_HEADS, c, c)),
                  _const_spec((c, WIDTH)),
                  _const_spec((c, WIDTH)),
                  _const_spec((WIDTH, 1)),
                  _const_spec((WIDTH, WIDTH)),
                  _const_spec((WIDTH, WIDTH))],
        out_specs=pl.BlockSpec((nb, c, WIDTH), lambda b, n: (b, n, 0)),
        scratch_shapes=[pltpu.VMEM((nb, WIDTH, WIDTH), F32)],
        compiler_params=_params("parallel", "arbitrary"),
        name="retention",
    )(mix_in, mix_in, mix_in, tabs["cos"], tabs["sin"], tabs["rot"], tabs["decay"], tabs["xi"],
      tabs["zeta"], tabs["cd"], tabs["bd"], tabs["avg"])


def _retention_tables(seq):
    c = RET_TILE
    half = HEAD_DIM // 2
    inv_freq = ROPE_BASE ** (-jnp.arange(half, dtype=F32) / half)
    ang = jnp.arange(seq, dtype=F32)[:, None] * inv_freq[None, :]
    cos = jnp.tile(jnp.cos(ang), (1, 2 * N_HEADS))
    sin = jnp.tile(jnp.sin(ang), (1, 2 * N_HEADS))
    lane = jnp.arange(WIDTH)
    d = lane % HEAD_DIM
    src = jnp.where(d < half, lane + half, lane - half)
    sign = jnp.where(d < half, -1.0, 1.0)
    rot = jnp.where(lane[:, None] == src[None, :], sign[None, :], 0.0).astype(BF16)
    log_g = jnp.log1p(-(2.0 ** (-5.0 - jnp.arange(N_HEADS, dtype=F32))))
    i = jnp.arange(c, dtype=F32)
    rel = i[:, None] - i[None, :]
    decay = jnp.where(rel >= 0, jnp.exp(log_g[:, None, None] * jnp.maximum(rel, 0.0)), 0.0)
    log_g_lane = log_g[lane // HEAD_DIM]
    xi = jnp.exp(log_g_lane[None, :] * (i + 1.0)[:, None])
    zeta = jnp.exp(log_g_lane[None, :] * (c - 1 - i)[:, None])
    cd = jnp.exp(log_g_lane * c)[:, None]
    same_head = (lane[:, None] // HEAD_DIM) == (lane[None, :] // HEAD_DIM)
    return {"cos": cos, "sin": sin, "rot": rot, "decay": decay, "xi": xi, "zeta": zeta, "cd": cd,
            "bd": same_head.astype(F32), "avg": (same_head.astype(F32) / HEAD_DIM).astype(BF16)}


def _s5_prep_body(a_re_ref, a_im_ref, log_dt_ref, b_re_ref, b_im_ref, lam_re_ref, lam_im_ref,
                  bb_re_ref, bb_im_ref):
    a_re = a_re_ref[...]
    a_im = a_im_ref[...]
    dt = jnp.exp(log_dt_ref[...])
    mag = jnp.exp(dt * a_re)
    ab_re = mag * jnp.cos(dt * a_im)
    ab_im = mag * jnp.sin(dt * a_im)
    den = a_re * a_re + a_im * a_im
    num_re = ab_re - 1.0
    f_re = (num_re * a_re + ab_im * a_im) / den
    f_im = (ab_im * a_re - num_re * a_im) / den
    b_re = b_re_ref[...]
    b_im = b_im_ref[...]
    lam_re_ref[...] = ab_re
    lam_im_ref[...] = ab_im
    bb_re_ref[...] = f_re * b_re - f_im * b_im
    bb_im_ref[...] = f_re * b_im + f_im * b_re


def _s5_prep(a_re, a_im, log_dt, b_re, b_im):
    rows = a_re.shape[0]
    col = jax.ShapeDtypeStruct((rows, 1), F32)
    mat = jax.ShapeDtypeStruct((rows, SSM_GROUP), F32)
    return pl.pallas_call(_s5_prep_body, out_shape=(col, col, mat, mat), name="s5_prep")(
        a_re, a_im, log_dt, b_re, b_im)


def _gelu_tanh(y):
    return 0.5 * y * (1.0 + jnp.tanh(math.sqrt(2.0 / math.pi) * (y + 0.044715 * (y * y * y))))


def _scan_body(us_ref, ul_ref, bbd_ref, lam_re_ref, lam_im_ref, c_re_ref, c_im_ref, dskip_ref, wglu_ref,
               bglu_ref, convw_ref, convb_ref, wa_ref, ba_ref, wx_ref, bx_ref, lam_lru_ref, shift_ref,
               ys_ref, yl_ref,
               x_scr, h_scr, hstate, us_scr, ul_scr, ab_scr, yl_scr, out_scr, lstate):
    batch, steps, _ = us_ref.shape
    n_slabs = SSM_LANES // LANES
    lru_slabs = WIDTH // LANES
    pitch = steps + SEQ_PAD
    slab_rows = x_scr.shape[1]
    lead = CONV_LEAD
    step_rows = lambda t: pl.ds(SEQ_PAD + t, batch, stride=pitch)
    seq_rows = lambda b: slice(b * pitch + SEQ_PAD, (b + 1) * pitch)

    @pl.when(pl.program_id(0) == 0)
    def _():
        hstate[...] = jnp.zeros_like(hstate)
        lstate[...] = jnp.zeros_like(lstate)
        us_scr[...] = jnp.zeros_like(us_scr)
        ul_scr[...] = jnp.zeros_like(ul_scr)
        h_scr[...] = jnp.zeros_like(h_scr)
        yl_scr[...] = jnp.zeros_like(yl_scr)

    for b in range(batch):
        us_scr[seq_rows(b), :] = us_ref[b].astype(F32)
        ul_scr[lead + b * pitch + SEQ_PAD:lead + (b + 1) * pitch, :] = ul_ref[b].astype(F32)
    us = us_scr[...]
    usb = us.astype(BF16)
    half = slab_rows // 2
    chunk = slab_rows // n_slabs
    neg_lam = -lam_lru_ref[...]
    softplus = jnp.maximum(neg_lam, 0.0) + jnp.log(1.0 + jnp.exp(-jnp.abs(neg_lam)))
    all_taps = [_dot(shift_ref[...], ul_scr[lead + q * chunk - CONV_LEAD:lead + q * chunk - CONV_LEAD + CONV_WINDOW, :].astype(BF16))
                for q in range(n_slabs)]
    for jj in range(n_slabs):
        for r0 in (0, half):
            part = _dot(usb[r0:r0 + half], bbd_ref[:, 2 * jj * LANES:(2 * jj + 2) * LANES])
            for j in range(2):
                x_scr[2 * jj + j, r0:r0 + half, :] = part[:, j * LANES:(j + 1) * LANES]
        r0 = jj * chunk
        taps = all_taps[jj]
        xc = convb_ref[...] + convw_ref[CONV_WIDTH - 1:CONV_WIDTH, :] * ul_scr[lead + r0:lead + r0 + chunk, :]
        for kk in range(CONV_WIDTH - 1):
            xc = xc + convw_ref[kk:kk + 1, :] * taps[kk * chunk:(kk + 1) * chunk]
        xcb = xc.astype(BF16)
        r = _sigmoid(_dot(xcb, wa_ref[...]) + ba_ref[...])
        ig = _sigmoid(_dot(xcb, wx_ref[...]) + bx_ref[...])
        a = jnp.exp(-LRU_C * r * softplus)
        gap = 1.0 - a * a
        bterm = gap * lax.rsqrt(jnp.maximum(gap, TINY)) * (ig * xc)
        for j in range(lru_slabs):
            ab_scr[j, r0:r0 + chunk, :] = a[:, j * LANES:(j + 1) * LANES]
            ab_scr[lru_slabs + j, r0:r0 + chunk, :] = bterm[:, j * LANES:(j + 1) * LANES]
    for b in range(batch):
        ul_scr[lead + b * pitch:lead + b * pitch + SEQ_PAD, :] = (
            ul_scr[lead + (b + 1) * pitch - SEQ_PAD:lead + (b + 1) * pitch, :])

    for c0 in range(0, n_slabs, SCAN_SLABS):
        lam = [(lam_re_ref[:, (c0 + s) * LANES:(c0 + s + 1) * LANES],
                lam_im_ref[:, (c0 + s) * LANES:(c0 + s + 1) * LANES]) for s in range(SCAN_SLABS)]
        with_lru = c0 == 0

        def step(t, carry, c0=c0, lam=lam, with_lru=with_lru):
            new = []
            for s in range(SCAN_SLABS):
                h_re, h_im = carry[2 * s], carry[2 * s + 1]
                lam_re, lam_im = lam[s]
                n_re = lam_re * h_re - lam_im * h_im + x_scr.at[c0 + s][step_rows(t), :]
                n_im = lam_re * h_im + lam_im * h_re + x_scr.at[n_slabs + c0 + s][step_rows(t), :]
                h_scr.at[c0 + s][step_rows(t), :] = n_re
                h_scr.at[n_slabs + c0 + s][step_rows(t), :] = n_im
                new += [n_re, n_im]
            if with_lru:
                for s in range(lru_slabs):
                    h = (ab_scr.at[s][step_rows(t), :] * carry[2 * SCAN_SLABS + s]
                         + ab_scr.at[lru_slabs + s][step_rows(t), :])
                    yl_scr.at[s][step_rows(t), :] = h
                    new.append(h)
            return tuple(new)

        init = []
        for s in range(SCAN_SLABS):
            init += [hstate[c0 + s], hstate[n_slabs + c0 + s]]
        if with_lru:
            init += [lstate[s] for s in range(lru_slabs)]
        final = lax.fori_loop(0, steps, step, tuple(init), unroll=2)
        for s in range(SCAN_SLABS):
            hstate[c0 + s] = final[2 * s]
            hstate[n_slabs + c0 + s] = final[2 * s + 1]
        if with_lru:
            for s in range(lru_slabs):
                lstate[s] = final[2 * SCAN_SLABS + s]

    h_all = jnp.concatenate([h_scr[j].astype(BF16) for j in range(2 * n_slabs)], axis=-1)
    c_all = jnp.concatenate([c_re_ref[...], -c_im_ref[...]], axis=0)
    y = jnp.concatenate([_dot(h_all[r0:r0 + half], c_all) for r0 in (0, half)], axis=0)
    y = _gelu_tanh(y + dskip_ref[...] * us)
    out_scr[0] = y * _sigmoid(_dot(y.astype(BF16), wglu_ref[...]) + bglu_ref[...])
    out_scr[1] = jnp.concatenate([yl_scr[j] for j in range(lru_slabs)], axis=-1)
    for b in range(batch):
        ys_ref[b] = out_scr[0, seq_rows(b), :].astype(ys_ref.dtype)
        yl_ref[b] = out_scr[1, seq_rows(b), :].astype(yl_ref.dtype)


def _recurrent(mix_in, p):
    batch, seq, _ = mix_in.shape
    steps = SCAN_STEPS
    slab_rows = -(-batch * (steps + SEQ_PAD) // LANES) * LANES
    vec = lambda: _const_spec((1, WIDTH))
    sq = lambda: _const_spec((WIDTH, WIDTH))
    out = jax.ShapeDtypeStruct((batch, seq, WIDTH), BF16)
    return pl.pallas_call(
        _scan_body,
        out_shape=(out, out),
        grid=(seq // steps,),
        in_specs=[pl.BlockSpec((batch, steps, WIDTH), lambda t: (0, t, 3)),
                  pl.BlockSpec((batch, steps, WIDTH), lambda t: (0, t, 7)),
                  _const_spec((WIDTH, 2 * SSM_LANES)),
                  _const_spec((batch, SSM_LANES)), _const_spec((batch, SSM_LANES)),
                  _const_spec((SSM_LANES, WIDTH)), _const_spec((SSM_LANES, WIDTH)),
                  vec(), sq(), vec(),
                  _const_spec((CONV_WIDTH, WIDTH)), vec(), sq(), vec(), sq(), vec(), vec(),
                  _const_spec(p["shift"].shape)],
        out_specs=(pl.BlockSpec((batch, steps, WIDTH), lambda t: (0, t, 0)),
                   pl.BlockSpec((batch, steps, WIDTH), lambda t: (0, t, 0))),
        scratch_shapes=[pltpu.VMEM((2 * SSM_LANES // LANES, slab_rows, LANES), F32),
                        pltpu.VMEM((2 * SSM_LANES // LANES, slab_rows, LANES), F32),
                        pltpu.VMEM((2 * SSM_LANES // LANES, batch, LANES), F32),
                        pltpu.VMEM((slab_rows, WIDTH), F32),
                        pltpu.VMEM((CONV_LEAD + slab_rows + CONV_WINDOW, WIDTH), F32),
                        pltpu.VMEM((2 * WIDTH // LANES, slab_rows, LANES), F32),
                        pltpu.VMEM((WIDTH // LANES, slab_rows, LANES), F32),
                        pltpu.VMEM((2, slab_rows, WIDTH), F32),
                        pltpu.VMEM((WIDTH // LANES, batch, LANES), F32)],
        compiler_params=_params("arbitrary"),
        name="recurrent",
    )(mix_in, mix_in, p["bbd"], p["lam_re"], p["lam_im"], p["c_re"], p["c_im"], p["d_skip"], p["w_glu"],
      p["b_glu"], p["conv_w"], p["conv_b"], p["w_a"], p["b_a"], p["w_x"], p["b_x"], p["lam"], p["shift"])


def _merge_body(x_ref, pre_g_ref, post_g_ref, y0_ref, y1_ref, y2_ref, y3_ref, wg_ref, wm0_ref, wm1_ref, wm2_ref,
                wm3_ref, wb_ref, wo_ref, o_ref):
    x = x_ref[...]
    h = _rms_norm(x, pre_g_ref[...]).astype(BF16)
    y_refs = (y0_ref, y1_ref, y2_ref, y3_ref)
    wm_refs = (wm0_ref, wm1_ref, wm2_ref, wm3_ref)
    gates = [_dot(h, wg_ref[:, n * WIDTH:(n + 1) * WIDTH]) for n in range(N_BRANCH)]
    mixes = [(y_refs[n][...].astype(F32) * (gates[n] * _sigmoid(gates[n]))).astype(BF16) for n in range(N_BRANCH)]
    merged = None
    pre = _dot(h, wm_refs[0][...])
    for n in range(N_BRANCH):
        branch = _dot(mixes[n], wb_ref[n])
        nxt = _dot(h, wm_refs[n + 1][...]) if n + 1 < N_BRANCH else None
        term = _sigmoid(pre) * branch
        merged = term if merged is None else merged + term
        pre = nxt
    out = _dot(merged.astype(BF16), wo_ref[...])
    o_ref[...] = x + _rms_norm(out, post_g_ref[...])


def _merge(x2, pre_g, post_g, ys, w_l, w_branch, w_out):
    assert GATE_COLS == D_MODEL and MIX_COLS % D_MODEL == 0
    rows = x2.shape[0]
    row_spec = lambda w: pl.BlockSpec((ROW_TILE, w), lambda i: (i, 0))
    col_block = lambda j: pl.BlockSpec((D_MODEL, D_MODEL), lambda i: (0, j), pipeline_mode=pl.Buffered(1))
    first = MIX_COLS // D_MODEL
    return pl.pallas_call(
        _merge_body,
        out_shape=jax.ShapeDtypeStruct((rows, D_MODEL), F32),
        grid=(rows // ROW_TILE,),
        in_specs=[row_spec(D_MODEL), _const_spec((1, D_MODEL)), _const_spec((1, D_MODEL)),
                  row_spec(WIDTH), row_spec(WIDTH), row_spec(WIDTH), row_spec(WIDTH),
                  col_block(first),
                  *[col_block(first + 1 + n) for n in range(N_BRANCH)],
                  _const_spec((N_BRANCH, WIDTH, D_MODEL)),
                  _const_spec((D_MODEL, D_MODEL))],
        out_specs=row_spec(D_MODEL),
        compiler_params=_params("parallel"),
        name="merge",
    )(x2, pre_g, post_g, *ys, *([w_l] * (1 + N_BRANCH)), w_branch, w_out)


def _block_diag(blocks):
    n, r, c = blocks.shape
    eye = jnp.eye(n, dtype=blocks.dtype)
    return (blocks[:, :, None, :] * eye[:, None, :, None]).reshape(n * r, n * c)


def kernel(x, pre_norm_g, post_norm_g, w_in, ssm_a_re, ssm_a_im, ssm_log_dt, ssm_b_re, ssm_b_im, ssm_c_re,
           ssm_c_im, ssm_d, ssm_w_glu, ssm_b_glu, lru_conv_w, lru_conv_b, lru_w_a, lru_b_a, lru_w_x, lru_b_x,
           lru_lambda, w_branch, w_out):
    batch, seq, _ = x.shape
    depth = w_in.shape[0]
    g, n, c = SSM_GROUPS, SSM_STATE, SSM_GROUP
    assert batch % (2 * SUBLANES) == 0 and batch % ATT_BATCH == 0 and batch % RET_BATCH == 0
    assert seq % max(ATT_TILE, RET_TILE, SCAN_STEPS) == 0 and (batch * seq) % ROW_TILE == 0

    tri = jnp.tril(jnp.ones((ATT_TILE, ATT_TILE), F32), -1).astype(BF16)
    ret_tabs = _retention_tables(seq)
    chunk = -(-batch * (SCAN_STEPS + SEQ_PAD) // LANES) * LANES // (SSM_LANES // LANES)
    tap_row = (jnp.arange(chunk)[None, :] + CONV_LEAD - (CONV_WIDTH - 1) + jnp.arange(CONV_WIDTH - 1)[:, None])
    shift = (tap_row.reshape(-1, 1) == jnp.arange(CONV_WINDOW)[None, :]).astype(BF16)

    flat = lambda a: a.reshape(depth * g * n, -1)
    log_dt = jnp.broadcast_to(ssm_log_dt[:, :, None], (depth, g, n))
    lam_re, lam_im, bb_re, bb_im = _s5_prep(flat(ssm_a_re), flat(ssm_a_im), flat(log_dt),
                                             flat(ssm_b_re), flat(ssm_b_im))

    rows = batch * seq
    x2 = x.reshape(rows, D_MODEL)
    for l in range(depth):
        w_l = w_in[l].astype(BF16)
        bb = lambda a: _block_diag(jnp.transpose(a.reshape(depth, g, n, c)[l], (0, 2, 1)))
        lam_rows = lambda a: jnp.broadcast_to(a.reshape(depth, 1, g * n)[l], (batch, g * n))
        rec = {
            "bbd": jnp.concatenate([bb(bb_re), bb(bb_im)], axis=1).astype(BF16),
            "lam_re": lam_rows(lam_re), "lam_im": lam_rows(lam_im),
            "c_re": _block_diag(jnp.transpose(ssm_c_re[l], (0, 2, 1))).astype(BF16),
            "c_im": _block_diag(jnp.transpose(ssm_c_im[l], (0, 2, 1))).astype(BF16),
            "d_skip": ssm_d[l][None, :], "w_glu": ssm_w_glu[l].astype(BF16), "b_glu": ssm_b_glu[l][None, :],
            "conv_w": lru_conv_w[l], "conv_b": lru_conv_b[l][None, :],
            "w_a": _block_diag(lru_w_a[l]).astype(BF16), "b_a": lru_b_a[l].reshape(1, WIDTH),
            "w_x": _block_diag(lru_w_x[l]).astype(BF16), "b_x": lru_b_x[l].reshape(1, WIDTH),
            "lam": lru_lambda[l][None, :], "shift": shift,
        }
        pre_g = pre_norm_g[l][None, :]
        mix_in = _inproj(x2, pre_g, w_l).reshape(batch, seq, MIX_COLS)
        y_sb = _stick_breaking(mix_in, tri)
        y_ret = _retention(mix_in, ret_tabs)
        y_ssm, y_lru = _recurrent(mix_in, rec)
        ys = tuple(y.reshape(rows, WIDTH) for y in (y_sb, y_ssm, y_ret, y_lru))
        x2 = _merge(x2, pre_g, post_norm_g[l][None, :], ys, w_l, w_branch[l].astype(BF16), w_out[l].astype(BF16))
    return x2.reshape(batch, seq, D_MODEL)
```
